```python
import math
import jax, jax.numpy as jnp
from jax import lax
import numpy as np

D_MODEL = 1024
BATCH = 4
SEQ = 4096
DEPTH = 1

CHUNK = 64
RMS_EPS = 1e-6
RW_HEADS = 8
RW_HEAD_DIM = 64
RW_WIDTH = RW_HEADS * RW_HEAD_DIM
RW_DECAY_LORA = 64
RW_AAA_LORA = 64
RW_GATE_LORA = 128
RW_GN_EPS = 64e-5
ML_HEADS = 4
ML_HEAD_DIM = 128
ML_WIDTH = ML_HEADS * ML_HEAD_DIM
ML_CONV = 4
ML_NORM_EPS = 1e-5
N_BRANCH = 2
PEER_HEADS = 8
PEER_KEYS = 128
PEER_EXPERTS = PEER_KEYS * PEER_KEYS
PEER_QDIM = 256
PEER_HALF = PEER_QDIM // 2
PEER_TOPK = 16
PEER_TOKEN_BLOCK = 128
RW_COLS = 3 * RW_WIDTH + RW_DECAY_LORA + RW_AAA_LORA + RW_GATE_LORA
ML_COLS = 4 * ML_WIDTH + 2 * ML_HEADS
GATE_COLS = N_BRANCH * D_MODEL
IN_COLS = RW_COLS + ML_COLS + GATE_COLS

kernel_name = "hybrid_rwkv7_mlstm_peer_block"

F32 = jnp.float32


def _rmsnorm(x, g):
    xf = x.astype(F32)
    y = xf * lax.rsqrt(jnp.mean(xf * xf, axis=-1, keepdims=True) + RMS_EPS)
    return (y * g.astype(F32)).astype(x.dtype)


def _causal_conv(x, w, b):
    c = x.shape[-1]
    y = lax.conv_general_dilated(x, w[:, None, :], window_strides=(1,), padding=[(w.shape[0] - 1, 0)],
                                 dimension_numbers=('NWC', 'WIO', 'NWC'), feature_group_count=c)
    return y + b


def _rwkv7(z, mu, w0, w_up, a0, a_up, g_up, k_k, k_a, r_k, gn_w, gn_b):
    bsz, t_len, _ = z.shape
    z = z.astype(F32)
    z_prev = jnp.pad(z, ((0, 0), (1, 0), (0, 0)))[:, :-1]
    zs = z + (z_prev - z) * mu.astype(F32)
    o1 = RW_WIDTH; o2 = 2 * RW_WIDTH; o3 = 3 * RW_WIDTH
    o4 = o3 + RW_DECAY_LORA; o5 = o4 + RW_AAA_LORA
    r, k, v = zs[..., :o1], zs[..., o1:o2], zs[..., o2:o3]
    zw, za, zg = zs[..., o3:o4], zs[..., o4:o5], zs[..., o5:]
    w_log = -jax.nn.softplus(-(w0.astype(F32) + jnp.tanh(zw) @ w_up.astype(F32))) - 0.5
    decay = jnp.exp(-jnp.exp(w_log))
    a = jax.nn.sigmoid(a0.astype(F32) + za @ a_up.astype(F32))
    g = jax.nn.sigmoid(zg) @ g_up.astype(F32)
    hs = lambda t: t.reshape(bsz, t_len, RW_HEADS, RW_HEAD_DIM)
    r, k, v, decay, a = hs(r), hs(k), hs(v), hs(decay), hs(a)
    kk = k * k_k.astype(F32).reshape(RW_HEADS, RW_HEAD_DIM)
    kk = kk / jnp.maximum(jnp.sqrt(jnp.sum(kk * kk, axis=-1, keepdims=True)), 1e-12)
    k = k * (1.0 + (a - 1.0) * k_a.astype(F32).reshape(RW_HEADS, RW_HEAD_DIM))

    def step(S, inp):
        r_t, w_t, k_t, v_t, kk_t, b_t = inp
        sa = jnp.einsum('bhvk,bhk->bhv', S, -kk_t)
        S = S * w_t[:, :, None, :] + sa[..., None] * b_t[:, :, None, :] + v_t[..., None] * k_t[:, :, None, :]
        return S, jnp.einsum('bhvk,bhk->bhv', S, r_t)

    tm = lambda t: jnp.moveaxis(t, 1, 0)
    s0 = jnp.zeros((bsz, RW_HEADS, RW_HEAD_DIM, RW_HEAD_DIM), F32)
    _, y = lax.scan(step, s0, (tm(r), tm(decay), tm(k), tm(v), tm(kk), tm(kk * a)))
    y = jnp.moveaxis(y, 0, 1)
    mean = jnp.mean(y, axis=-1, keepdims=True)
    var = jnp.mean(jnp.square(y - mean), axis=-1, keepdims=True)
    y = (y - mean) * lax.rsqrt(var + RW_GN_EPS) * gn_w.astype(F32).reshape(RW_HEADS, RW_HEAD_DIM) \
        + gn_b.astype(F32).reshape(RW_HEADS, RW_HEAD_DIM)
    y = y + jnp.sum(r * k * r_k.astype(F32), axis=-1, keepdims=True) * v
    return y.reshape(bsz, t_len, RW_WIDTH) * g


def _mlstm(z, cq_w, cq_b, ck_w, ck_b, b_i, b_f, norm_w):
    bsz, t_len, _ = z.shape
    z = z.astype(F32)
    w1 = ML_WIDTH; w2 = 2 * ML_WIDTH; w3 = 3 * ML_WIDTH; w4 = 4 * ML_WIDTH
    q_pre, k_pre, v, o_pre = z[..., :w1], z[..., w1:w2], z[..., w2:w3], z[..., w3:w4]
    i_pre, f_pre = z[..., w4:w4 + ML_HEADS], z[..., w4 + ML_HEADS:]
    q = jax.nn.silu(_causal_conv(q_pre, cq_w.astype(F32), cq_b.astype(F32)))
    k = jax.nn.silu(_causal_conv(k_pre, ck_w.astype(F32), ck_b.astype(F32)))
    n_chunks = t_len // CHUNK
    chunked = lambda t: t.reshape(bsz, n_chunks, CHUNK, ML_HEADS, ML_HEAD_DIM).transpose(0, 3, 1, 2, 4)
    gate_c = lambda t: t.reshape(bsz, n_chunks, CHUNK, ML_HEADS).transpose(0, 3, 1, 2)
    q = chunked(q)
    k = chunked(k) * (ML_HEAD_DIM ** -0.5)
    v = chunked(v)
    ig = gate_c(i_pre + b_i.astype(F32))
    lf = jax.nn.log_sigmoid(gate_c(f_pre + b_f.astype(F32)))
    bcum = jnp.cumsum(lf, axis=-1)
    causal = jnp.tril(jnp.ones((CHUNK, CHUNK), dtype=bool))
    d_log = jnp.where(causal, bcum[..., :, None] - bcum[..., None, :] + ig[..., None, :], -jnp.inf)
    g_end = bcum[..., -1]
    a_s = g_end[..., None] - bcum + ig
    m_loc = jnp.max(a_s, axis=-1)
    wts = jnp.exp(a_s - m_loc[..., None])
    c_loc = jnp.einsum('bhcl,bhclv,bhclk->bhcvk', wts, v, k)
    n_loc = jnp.einsum('bhcl,bhclk->bhck', wts, k)

    def step(carry, inp):
        c_st, n_st, m_st = carry
        c_l, n_l, m_l, g_c = inp
        m_new = jnp.maximum(g_c + m_st, m_l)
        s_old = jnp.exp(g_c + m_st - m_new)
        s_new = jnp.exp(m_l - m_new)
        c_next = s_old[..., None, None] * c_st + s_new[..., None, None] * c_l
        n_next = s_old[..., None] * n_st + s_new[..., None] * n_l
        return (c_next, n_next, m_new), (c_st, n_st, m_st)

    init = (jnp.zeros((bsz, ML_HEADS, ML_HEAD_DIM, ML_HEAD_DIM), F32),
            jnp.zeros((bsz, ML_HEADS, ML_HEAD_DIM), F32),
            jnp.zeros((bsz, ML_HEADS), F32))
    mv = lambda t: jnp.moveaxis(t, 2, 0)
    _, (c_in, n_in, m_in) = lax.scan(step, init, (mv(c_loc), mv(n_loc), mv(m_loc), mv(g_end)))
    c_in = jnp.moveaxis(c_in, 0, 2)
    n_in = jnp.moveaxis(n_in, 0, 2)
    m_in = jnp.moveaxis(m_in, 0, 2)
    inter_log = bcum + m_in[..., None]
    m_t = jnp.maximum(jnp.max(d_log, axis=-1), inter_log)
    scores = jnp.einsum('bhcld,bhcsd->bhcls', q, k) * jnp.exp(d_log - m_t[..., None])
    inter_w = jnp.exp(inter_log - m_t)
    num = jnp.einsum('bhcls,bhcsd->bhcld', scores, v) + inter_w[..., None] * jnp.einsum('bhcvk,bhclk->bhclv', c_in, q)
    den = jnp.sum(scores, axis=-1) + inter_w * jnp.einsum('bhck,bhclk->bhcl', n_in, q)
    h = num / jnp.maximum(jnp.abs(den), jnp.exp(-m_t))[..., None]
    h = h.transpose(0, 2, 3, 1, 4).reshape(bsz, t_len, ML_HEADS, ML_HEAD_DIM)
    mean = jnp.mean(h, axis=-1, keepdims=True)
    var = jnp.mean(jnp.square(h - mean), axis=-1, keepdims=True)
    h = (h - mean) * lax.rsqrt(var + ML_NORM_EPS)
    h = h.reshape(bsz, t_len, ML_WIDTH) * norm_w.astype(F32)
    return h * jax.nn.sigmoid(o_pre)


def _peer(h, w_q, sub_keys, u_emb, v_emb):
    bsz, t_len, d = h.shape
    tok = h.reshape(-1, d)
    n_tok = tok.shape[0]
    q = (tok @ w_q).astype(F32).reshape(n_tok, PEER_HEADS, 2, PEER_HALF)
    s = jnp.einsum('nhpc,hpkc->nhpk', q, sub_keys.astype(F32))
    s_top, i_top = lax.top_k(s, PEER_TOPK)
    cand = (s_top[:, :, 0, :, None] + s_top[:, :, 1, None, :]).reshape(n_tok, PEER_HEADS, PEER_TOPK * PEER_TOPK)
    cand_idx = (i_top[:, :, 0, :, None] * PEER_KEYS + i_top[:, :, 1, None, :]).reshape(n_tok, PEER_HEADS, PEER_TOPK * PEER_TOPK)
    best, pos = lax.top_k(cand, PEER_TOPK)
    expert = jnp.take_along_axis(cand_idx, pos, axis=-1)
    gate = jax.nn.softmax(best, axis=-1)
    n_blk = n_tok // PEER_TOKEN_BLOCK
    sel = PEER_HEADS * PEER_TOPK

    def retrieve(args):
        tb, eb, gb = args
        act = jax.nn.gelu(jnp.einsum('tkd,td->tk', u_emb[eb], tb), approximate=False) * gb
        return jnp.einsum('tk,tkd->td', act, v_emb[eb])

    out = lax.map(retrieve, (tok.reshape(n_blk, PEER_TOKEN_BLOCK, d),
                             expert.reshape(n_blk, PEER_TOKEN_BLOCK, sel),
                             gate.astype(tok.dtype).reshape(n_blk, PEER_TOKEN_BLOCK, sel)))
    return out.reshape(bsz, t_len, d)


def setup_inputs(seed: int = 0) -> dict:
    key = jax.random.key(seed)
    ks = jax.random.split(key, 40)
    nrm = lambda k, shape, scale: jax.random.normal(k, shape, F32) * scale
    L = DEPTH
    return {
        "x": nrm(ks[0], (BATCH, SEQ, D_MODEL), 1.0),
        "norm1_g": 1.0 + nrm(ks[1], (L, D_MODEL), 0.02),
        "w_in": nrm(ks[2], (L, D_MODEL, IN_COLS), D_MODEL ** -0.5),
        "rw_mu": jax.random.uniform(ks[3], (L, RW_COLS), F32, 0.0, 1.0),
        "rw_w0": jnp.broadcast_to(jnp.linspace(-6.0, -1.0, RW_WIDTH, dtype=F32), (L, RW_WIDTH)) + nrm(ks[4], (L, RW_WIDTH), 0.1),
        "rw_w_up": nrm(ks[5], (L, RW_DECAY_LORA, RW_WIDTH), 0.1),
        "rw_a0": nrm(ks[6], (L, RW_WIDTH), 0.1),
        "rw_a_up": nrm(ks[7], (L, RW_AAA_LORA, RW_WIDTH), 0.1),
        "rw_g_up": nrm(ks[8], (L, RW_GATE_LORA, RW_WIDTH), RW_GATE_LORA ** -0.5),
        "rw_k_k": 0.85 + nrm(ks[9], (L, RW_WIDTH), 0.02),
        "rw_k_a": 1.0 + nrm(ks[10], (L, RW_WIDTH), 0.02),
        "rw_r_k": nrm(ks[11], (L, RW_HEADS, RW_HEAD_DIM), 0.1),
        "rw_gn_w": 1.0 + nrm(ks[12], (L, RW_WIDTH), 0.02),
        "rw_gn_b": nrm(ks[13], (L, RW_WIDTH), 0.02),
        "ml_conv_q_w": nrm(ks[14], (L, ML_CONV, ML_WIDTH), 0.5),
        "ml_conv_q_b": nrm(ks[15], (L, ML_WIDTH), 0.02),
        "ml_conv_k_w": nrm(ks[16], (L, ML_CONV, ML_WIDTH), 0.5),
        "ml_conv_k_b": nrm(ks[17], (L, ML_WIDTH), 0.02),
        "ml_b_i": nrm(ks[18], (L, ML_HEADS), 0.1),
        "ml_b_f": jnp.broadcast_to(jnp.linspace(3.0, 6.0, ML_HEADS, dtype=F32), (L, ML_HEADS)) + nrm(ks[19], (L, ML_HEADS), 0.1),
        "ml_norm_w": 1.0 + nrm(ks[20], (L, ML_WIDTH), 0.02),
        "gate_b": nrm(ks[21], (L, GATE_COLS), 0.02),
        "p_rw": nrm(ks[22], (L, RW_WIDTH, D_MODEL), RW_WIDTH ** -0.5),
        "p_ml": nrm(ks[23], (L, ML_WIDTH, D_MODEL), ML_WIDTH ** -0.5),
        "w_out": nrm(ks[24], (L, D_MODEL, D_MODEL), D_MODEL ** -0.5),
        "norm2_g": 1.0 + nrm(ks[25], (L, D_MODEL), 0.02),
        "peer_w_q": nrm(ks[26], (L, D_MODEL, PEER_HEADS * PEER_QDIM), D_MODEL ** -0.5),
        "peer_sub_keys": nrm(ks[27], (L, PEER_HEADS, 2, PEER_KEYS, PEER_HALF), PEER_HALF ** -0.5),
        "peer_u": nrm(ks[28], (L, PEER_EXPERTS, D_MODEL), D_MODEL ** -0.5),
        "peer_v": nrm(ks[29], (L, PEER_EXPERTS, D_MODEL), 0.25),
        "final_g": 1.0 + nrm(ks[30], (D_MODEL,), 0.02),
    }


def reference(x, norm1_g, w_in, rw_mu, rw_w0, rw_w_up, rw_a0, rw_a_up, rw_g_up, rw_k_k, rw_k_a, rw_r_k,
              rw_gn_w, rw_gn_b, ml_conv_q_w, ml_conv_q_b, ml_conv_k_w, ml_conv_k_b, ml_b_i, ml_b_f, ml_norm_w,
              gate_b, p_rw, p_ml, w_out, norm2_g, peer_w_q, peer_sub_keys, peer_u, peer_v, final_g):
    bsz, t_len, d = x.shape
    for l in range(DEPTH):
        h = _rmsnorm(x, norm1_g[l])
        z = h @ w_in[l]
        z_rw = z[..., :RW_COLS]
        z_ml = z[..., RW_COLS:RW_COLS + ML_COLS]
        z_gate = z[..., RW_COLS + ML_COLS:]
        y_rw = _rwkv7(z_rw, rw_mu[l], rw_w0[l], rw_w_up[l], rw_a0[l], rw_a_up[l], rw_g_up[l],
                      rw_k_k[l], rw_k_a[l], rw_r_k[l], rw_gn_w[l], rw_gn_b[l]).astype(x.dtype)
        y_ml = _mlstm(z_ml, ml_conv_q_w[l], ml_conv_q_b[l], ml_conv_k_w[l], ml_conv_k_b[l],
                      ml_b_i[l], ml_b_f[l], ml_norm_w[l]).astype(x.dtype)
        gate = jax.nn.sigmoid((z_gate + gate_b[l]).astype(F32)).astype(x.dtype).reshape(bsz, t_len, N_BRANCH, d)
        merged = gate[:, :, 0, :] * (y_rw @ p_rw[l]) + gate[:, :, 1, :] * (y_ml @ p_ml[l])
        x = x + merged @ w_out[l]
        x = x + _peer(_rmsnorm(x, norm2_g[l]), peer_w_q[l], peer_sub_keys[l], peer_u[l], peer_v[l])
    return _rmsnorm(x, final_g)
```

```python
import functools
import math

import jax
import jax.numpy as jnp
from jax import lax
from jax.experimental import pallas as pl
from jax.experimental.pallas import tpu as pltpu

F32 = jnp.float32
BF16 = jnp.bfloat16
HI = lax.Precision.HIGHEST

D_MODEL = 1024
CHUNK = 64
RMS_EPS = 1e-6
RW_HEADS = 8
RW_HEAD_DIM = 64
RW_WIDTH = 512
RW_DECAY_LORA = 64
RW_AAA_LORA = 64
RW_GATE_LORA = 128
RW_GN_EPS = 64e-5
RW_COLS = 3 * RW_WIDTH + RW_DECAY_LORA + RW_AAA_LORA + RW_GATE_LORA
RW_PAIRS = RW_HEADS // 2
ML_HEADS = 4
ML_HEAD_DIM = 128
ML_WIDTH = 512
ML_CONV = 4
ML_NORM_EPS = 1e-5
ML_MAIN = 4 * ML_WIDTH
PEER_HEADS = 8
PEER_KEYS = 128
PEER_EXPERTS = PEER_KEYS * PEER_KEYS
PEER_QDIM = 256
PEER_HALF = 128
PEER_TOPK = 16

LANES = 128
VMEM_LIMIT = 56 * 1024 * 1024

NEG_INF = float("-inf")


def _dot(a, b, precision=None):
    return jnp.dot(a, b, preferred_element_type=F32, precision=precision)


def _dot_nt(a, b, precision=None):
    return lax.dot_general(a, b, (((1,), (1,)), ((), ())), preferred_element_type=F32, precision=precision)


def _dot_tn(a, b, precision=None):
    return lax.dot_general(a, b, (((0,), (0,)), ((), ())), preferred_element_type=F32, precision=precision)


def _sigmoid(x):
    return 1.0 / (1.0 + jnp.exp(-x))


def _softplus(x):
    return jnp.maximum(x, 0.0) + jnp.log(1.0 + jnp.exp(-jnp.abs(x)))


def _iota(shape, dim):
    return lax.broadcasted_iota(jnp.int32, shape, dim)


def _inproj_kernel(x_ref, g_ref, wrw_ref, wml_ref, wif_ref, wgt_ref, zrw_ref, zml_ref, zif_ref, zgt_ref):
    x = x_ref[...]
    h = x * lax.rsqrt(jnp.mean(x * x, axis=-1, keepdims=True) + RMS_EPS) * g_ref[...]
    hb = h.astype(BF16)
    zrw_ref[...] = _dot(hb, wrw_ref[...])
    zml_ref[...] = _dot(hb, wml_ref[...])
    zif_ref[...] = _dot(hb, wif_ref[...])
    zgt_ref[...] = _dot(hb, wgt_ref[...])


def _inproj(x2, g, wrw, wml, wif, wgt, tm=256):
    n = x2.shape[0]
    row = lambda w: pl.BlockSpec((tm, w), lambda i: (i, 0))
    full = lambda a: pl.BlockSpec(a.shape, lambda i: (0,) * a.ndim)
    return pl.pallas_call(
        _inproj_kernel,
        grid=(n // tm,),
        in_specs=[row(D_MODEL), full(g), full(wrw), full(wml), full(wif), full(wgt)],
        out_specs=[row(wrw.shape[1]), row(wml.shape[1]), row(wif.shape[1]), row(wgt.shape[1])],
        out_shape=[jax.ShapeDtypeStruct((n, w.shape[1]), F32) for w in (wrw, wml, wif, wgt)],
        compiler_params=pltpu.CompilerParams(dimension_semantics=("parallel",), vmem_limit_bytes=VMEM_LIMIT),
        name="inproj",
    )(x2, g, wrw, wml, wif, wgt)


def _bd(xw):
    left = _iota(xw.shape, 1) < RW_HEAD_DIM
    return jnp.concatenate([jnp.where(left, xw, 0.0), jnp.where(left, 0.0, xw)], axis=0)


def _mm(xw, yw, precision=None):
    if precision is None:
        return _dot(xw.astype(BF16), _bd(yw).astype(BF16))
    return _dot(xw, _bd(yw), precision)


def _rwkv_kernel(z_ref, mu_ref, w0_ref, wup_ref, a0_ref, aup_ref, gup_ref, kk_ref, ka_ref, rk_ref,
                 gnw_ref, gnb_ref, y_ref, prev_ref, st_ref):
    c = pl.program_id(1)

    @pl.when(c == 0)
    def _():
        prev_ref[...] = jnp.zeros_like(prev_ref)
        st_ref[...] = jnp.zeros_like(st_ref)

    z = z_ref[...]
    rolled = pltpu.roll(z, 1, 0)
    first = _iota(z.shape, 0) == 0
    z_prev = jnp.where(first, jnp.broadcast_to(prev_ref[0:1, :], z.shape), rolled)
    prev_ref[0:1, :] = z[CHUNK - 1:CHUNK, :]
    zs = z + (z_prev - z) * mu_ref[...]

    o1, o2, o3 = RW_WIDTH, 2 * RW_WIDTH, 3 * RW_WIDTH
    r = zs[:, :o1]
    k = zs[:, o1:o2]
    v = zs[:, o2:o3]
    zwa = zs[:, o3:o3 + LANES]
    zg = zs[:, o3 + LANES:]
    w_log = -_softplus(-(w0_ref[...] + _dot(jnp.tanh(zwa), wup_ref[...], HI))) - 0.5
    logd = -jnp.exp(w_log)
    a = _sigmoid(a0_ref[...] + _dot(zwa, aup_ref[...], HI))
    g = _dot(_sigmoid(zg), gup_ref[...], HI)
    kk_raw = k * kk_ref[...]
    k = k * (1.0 + (a - 1.0) * ka_ref[...])
    rkr = r * k * rk_ref[...]

    ii = _iota((CHUNK, LANES), 0)
    jj = _iota((CHUNK, LANES), 1) % RW_HEAD_DIM
    strict = jj < ii
    incl = jj <= ii
    eye_w = (jj == ii).astype(F32)
    same16 = (ii // 16) == (jj // 16)
    same32 = (ii // 32) == (jj // 32)
    tri = (_iota((CHUNK, CHUNK), 1) <= _iota((CHUNK, CHUNK), 0)).astype(F32)
    r2 = _iota((LANES, LANES), 0)
    c2 = _iota((LANES, LANES), 1)
    blk = (r2 // RW_HEAD_DIM) == (c2 // RW_HEAD_DIM)
    ones_bd = blk.astype(F32)
    eye2 = r2 == c2

    ys = []
    for p in range(RW_PAIRS):
        sl = slice(p * LANES, (p + 1) * LANES)
        r_p, k_p, v_p, a_p, ld_p = r[:, sl], k[:, sl], v[:, sl], a[:, sl], logd[:, sl]
        kk_p = kk_raw[:, sl]
        ss = _dot(kk_p * kk_p, ones_bd, HI)
        kk_p = kk_p / jnp.maximum(jnp.sqrt(ss), 1e-12)
        bonus = _dot(rkr[:, sl], ones_bd, HI)

        lcum = _dot(tri, ld_p, HI)
        l_end = lcum[CHUNK - 1:CHUNK, :]
        p_inc = jnp.exp(lcum)
        p_inv = jnp.exp(-lcum)
        rt = r_p * p_inc
        kt = kk_p * jnp.exp(lcum - ld_p)
        kh = k_p * p_inv
        bh = kk_p * a_p * p_inv
        to_end = jnp.exp(l_end - lcum)
        kh_end = k_p * to_end
        bh_end = kk_p * a_p * to_end

        lhs = jnp.concatenate([kt, rt], axis=0).astype(BF16)
        rhs = jnp.concatenate([_bd(kh), _bd(bh)], axis=0).astype(BF16)
        sc = _dot_nt(lhs, rhs)
        a_vk = jnp.where(strict, sc[:CHUNK, :LANES], 0.0)
        n_w = jnp.where(strict, sc[:CHUNK, LANES:], 0.0)
        a_rk = jnp.where(incl, sc[CHUNK:, :LANES], 0.0)
        a_rb = jnp.where(incl, sc[CHUNK:, LANES:], 0.0)

        d1 = jnp.where(same16, n_w, 0.0)
        d2 = _mm(d1, d1, HI)
        d4 = _mm(d2, d2, HI)
        d8 = _mm(d4, d4, HI)
        t = _mm(_mm(_mm(eye_w - d1, eye_w + d2, HI), eye_w + d4, HI), eye_w + d8, HI)
        off32 = jnp.where(jnp.logical_and(same32, jnp.logical_not(same16)), n_w, 0.0)
        t = t - _mm(_mm(t, off32, HI), t, HI)
        off64 = jnp.where(same32, 0.0, n_w)
        t = t - _mm(_mm(t, off64, HI), t, HI)

        av = _mm(a_vk, v_p)
        x_w = _mm(t, av)
        w_w = _mm(t, kt)

        st = st_ref[p]
        st_b = st.astype(BF16)
        u = -(_dot(w_w.astype(BF16), st_b) + x_w)
        y = _dot(rt.astype(BF16), st_b) + _mm(a_rk, v_p) + _mm(a_rb, u)
        d_st = _dot_tn(jnp.concatenate([kh_end, bh_end], axis=0).astype(BF16),
                       jnp.concatenate([v_p, u], axis=0).astype(BF16))
        p_col = jnp.sum(jnp.where(eye2, jnp.broadcast_to(jnp.exp(l_end), (LANES, LANES)), 0.0),
                        axis=1, keepdims=True)
        st_ref[p] = st * p_col + jnp.where(blk, d_st, 0.0)

        mean = _dot(y, ones_bd, HI) * (1.0 / RW_HEAD_DIM)
        yc = y - mean
        var = _dot(yc * yc, ones_bd, HI) * (1.0 / RW_HEAD_DIM)
        yn = yc * lax.rsqrt(var + RW_GN_EPS) * gnw_ref[:, sl] + gnb_ref[:, sl]
        ys.append((yn + bonus * v_p) * g[:, sl])

    y_ref[...] = jnp.concatenate(ys, axis=1)


def _rwkv(z_rw, mu, w0, wup, a0, aup, gup, k_k, k_a, r_k, gn_w, gn_b):
    bsz, t_len, _ = z_rw.shape
    full = lambda a: pl.BlockSpec(a.shape, lambda b, c: (0,) * a.ndim)
    params = (mu, w0, wup, a0, aup, gup, k_k, k_a, r_k, gn_w, gn_b)
    return pl.pallas_call(
        _rwkv_kernel,
        grid=(bsz, t_len // CHUNK),
        in_specs=[pl.BlockSpec((None, CHUNK, RW_COLS), lambda b, c: (b, c, 0))] + [full(a) for a in params],
        out_specs=pl.BlockSpec((None, CHUNK, RW_WIDTH), lambda b, c: (b, c, 0)),
        out_shape=jax.ShapeDtypeStruct((bsz, t_len, RW_WIDTH), F32),
        scratch_shapes=[pltpu.VMEM((8, RW_COLS), F32), pltpu.VMEM((RW_PAIRS, LANES, LANES), F32)],
        compiler_params=pltpu.CompilerParams(dimension_semantics=("parallel", "arbitrary"),
                                             vmem_limit_bytes=VMEM_LIMIT),
        name="rwkv7",
    )(z_rw, *params)


def _mlstm_kernel(z_ref, zif_ref, cw_ref, cb_ref, bif_ref, nw_ref, h_ref, conv_ref, c_ref, n_ref, m_ref):
    c = pl.program_id(1)

    @pl.when(c == 0)
    def _():
        conv_ref[0:8, :] = jnp.zeros((8, 2 * ML_WIDTH), F32)
        c_ref[...] = jnp.zeros_like(c_ref)
        n_ref[...] = jnp.zeros_like(n_ref)
        m_ref[...] = jnp.zeros_like(m_ref)

    conv_ref[8:8 + CHUNK, :] = z_ref[:, 0:2 * ML_WIDTH]
    acc = jnp.broadcast_to(cb_ref[...], (CHUNK, 2 * ML_WIDTH))
    for j in range(ML_CONV):
        off = 8 - (ML_CONV - 1) + j
        acc = acc + cw_ref[j:j + 1, :] * conv_ref[off:off + CHUNK, :]
    conv_ref[0:8, :] = conv_ref[CHUNK:CHUNK + 8, :]
    qk = acc * _sigmoid(acc)
    q = qk[:, :ML_WIDTH]
    k = qk[:, ML_WIDTH:] * (ML_HEAD_DIM ** -0.5)
    v = z_ref[:, 2 * ML_WIDTH:3 * ML_WIDTH]
    o_pre = z_ref[:, 3 * ML_WIDTH:4 * ML_WIDTH]

    gz = zif_ref[...] + bif_ref[...]
    lf = -_softplus(-gz)
    tri = (_iota((CHUNK, CHUNK), 1) <= _iota((CHUNK, CHUNK), 0))
    bcum = _dot(tri.astype(F32), lf, HI)
    gz_t = gz.T
    bcum_t = bcum.T

    hs = []
    for h in range(ML_HEADS):
        sl = slice(h * ML_HEAD_DIM, (h + 1) * ML_HEAD_DIM)
        q_h, k_h, v_h = q[:, sl], k[:, sl], v[:, sl]
        fh = ML_HEADS + h
        bc_col = bcum[:, fh:fh + 1]
        bc_row = bcum_t[fh:fh + 1, :]
        ig_col = gz[:, h:h + 1]
        ig_row = gz_t[h:h + 1, :]
        g_end = bcum[CHUNK - 1:CHUNK, fh:fh + 1]
        d_log = jnp.where(tri, bc_col - bc_row + ig_row, NEG_INF)
        a_row = g_end - bc_row + ig_row
        m_loc = jnp.max(a_row, axis=1, keepdims=True)
        wts = jnp.exp(g_end - bc_col + ig_col - m_loc)
        k_w = k_h * wts
        c_loc = _dot_tn(v_h.astype(BF16), k_w.astype(BF16))
        n_loc = jnp.sum(k_w, axis=0, keepdims=True)

        c_in = c_ref[h]
        n_in = n_ref[h, 0:1, :]
        m_in = m_ref[h, 0:1, 0:1]
        inter_log = bc_col + m_in
        m_t = jnp.maximum(jnp.max(d_log, axis=1, keepdims=True), inter_log)
        scores = _dot_nt(q_h.astype(BF16), k_h.astype(BF16)) * jnp.exp(d_log - m_t)
        inter_w = jnp.exp(inter_log - m_t)
        num = _dot(scores.astype(BF16), v_h.astype(BF16)) \
            + inter_w * _dot_nt(q_h.astype(BF16), c_in.astype(BF16))
        den = jnp.sum(scores, axis=1, keepdims=True) + inter_w * jnp.sum(q_h * n_in, axis=1, keepdims=True)
        hh = num / jnp.maximum(jnp.abs(den), jnp.exp(-m_t))
        mean = jnp.mean(hh, axis=1, keepdims=True)
        hc = hh - mean
        var = jnp.mean(hc * hc, axis=1, keepdims=True)
        hn = hc * lax.rsqrt(var + ML_NORM_EPS) * nw_ref[:, sl]
        hs.append(hn * _sigmoid(o_pre[:, sl]))

        m_new = jnp.maximum(g_end + m_in, m_loc)
        s_old = jnp.exp(g_end + m_in - m_new)
        s_new = jnp.exp(m_loc - m_new)
        c_ref[h] = s_old * c_in + s_new * c_loc
        n_ref[h] = jnp.broadcast_to(s_old * n_in + s_new * n_loc, (8, ML_HEAD_DIM))
        m_ref[h] = jnp.broadcast_to(m_new, (8, LANES))

    h_ref[...] = jnp.concatenate(hs, axis=1)


def _mlstm(z_ml, z_if, conv_w, conv_b, b_if, norm_w):
    bsz, t_len, _ = z_ml.shape
    full = lambda a: pl.BlockSpec(a.shape, lambda b, c: (0,) * a.ndim)
    params = (conv_w, conv_b, b_if, norm_w)
    return pl.pallas_call(
        _mlstm_kernel,
        grid=(bsz, t_len // CHUNK),
        in_specs=[pl.BlockSpec((None, CHUNK, ML_MAIN), lambda b, c: (b, c, 0)),
                  pl.BlockSpec((None, CHUNK, LANES), lambda b, c: (b, c, 0))] + [full(a) for a in params],
        out_specs=pl.BlockSpec((None, CHUNK, ML_WIDTH), lambda b, c: (b, c, 0)),
        out_shape=jax.ShapeDtypeStruct((bsz, t_len, ML_WIDTH), F32),
        scratch_shapes=[pltpu.VMEM((8 + CHUNK, 2 * ML_WIDTH), F32),
                        pltpu.VMEM((ML_HEADS, ML_HEAD_DIM, ML_HEAD_DIM), F32),
                        pltpu.VMEM((ML_HEADS, 8, ML_HEAD_DIM), F32),
                        pltpu.VMEM((ML_HEADS, 8, LANES), F32)],
        compiler_params=pltpu.CompilerParams(dimension_semantics=("parallel", "arbitrary"),
                                             vmem_limit_bytes=VMEM_LIMIT),
        name="mlstm",
    )(z_ml, z_if, *params)


def _merge_kernel(x_ref, yrw_ref, yml_ref, zg_ref, gb_ref, prw_ref, pml_ref, wout_ref, n2_ref, wq_ref,
                  x1_ref, tok_ref, q_ref):
    gate = _sigmoid(zg_ref[...] + gb_ref[...])
    m_rw = _dot(yrw_ref[...].astype(BF16), prw_ref[...])
    m_ml = _dot(yml_ref[...].astype(BF16), pml_ref[...])
    merged = gate[:, :D_MODEL] * m_rw + gate[:, D_MODEL:] * m_ml
    x1 = x_ref[...] + _dot(merged.astype(BF16), wout_ref[...])
    x1_ref[...] = x1
    tok = x1 * lax.rsqrt(jnp.mean(x1 * x1, axis=-1, keepdims=True) + RMS_EPS) * n2_ref[...]
    tok_b = tok.astype(BF16)
    tok_ref[...] = tok_b
    q_ref[...] = _dot(tok_b, wq_ref[...])


def _merge(x2, y_rw, y_ml, z_gate, gate_b, p_rw, p_ml, w_out, n2, w_q, tm=256):
    n = x2.shape[0]
    row = lambda w: pl.BlockSpec((tm, w), lambda i: (i, 0))
    full = lambda a: pl.BlockSpec(a.shape, lambda i: (0,) * a.ndim)
    qd = w_q.shape[1]
    return pl.pallas_call(
        _merge_kernel,
        grid=(n // tm,),
        in_specs=[row(D_MODEL), row(RW_WIDTH), row(ML_WIDTH), row(2 * D_MODEL), full(gate_b), full(p_rw),
                  full(p_ml), full(w_out), full(n2), full(w_q)],
        out_specs=[row(D_MODEL), row(D_MODEL), row(qd)],
        out_shape=[jax.ShapeDtypeStruct((n, D_MODEL), F32), jax.ShapeDtypeStruct((n, D_MODEL), BF16),
                   jax.ShapeDtypeStruct((n, qd), F32)],
        compiler_params=pltpu.CompilerParams(dimension_semantics=("parallel",), vmem_limit_bytes=VMEM_LIMIT),
        name="merge",
    )(x2, y_rw, y_ml, z_gate, gate_b, p_rw, p_ml, w_out, n2, w_q)


_CELLS = [(r0, r1) for r0 in range(PEER_TOPK) for r1 in range(PEER_TOPK) if (r0 + 1) * (r1 + 1) <= PEER_TOPK]
_CELL_ROWS = 56


def _top16(s):
    kiota = _iota(s.shape, 0)
    rank = jnp.full(s.shape, float(PEER_KEYS - 1), F32)
    tops = []
    for r in range(PEER_TOPK):
        m = jnp.max(s, axis=0, keepdims=True)
        idx = jnp.min(jnp.where(s == m, kiota, PEER_KEYS), axis=0, keepdims=True)
        hit = kiota == idx
        rank = jnp.where(hit, float(r), rank)
        s = jnp.where(hit, NEG_INF, s)
        tops.append(m)
    return rank, tops


def _select_kernel(q_ref, sk_ref, a_ref, l_ref, b_ref, c_ref):
    tn = q_ref.shape[0]
    pos_col = jnp.concatenate(
        [jnp.full((1, 1), r0 * PEER_TOPK + r1, jnp.int32) for r0, r1 in _CELLS]
        + [jnp.full((_CELL_ROWS - len(_CELLS), 1), 1 << 20, jnp.int32)], axis=0)
    pos = jnp.broadcast_to(pos_col, (_CELL_ROWS, tn))
    for h in range(PEER_HEADS):
        base = h * PEER_QDIM
        s0 = _dot_nt(sk_ref[h, 0], q_ref[:, base:base + PEER_HALF])
        s1 = _dot_nt(sk_ref[h, 1], q_ref[:, base + PEER_HALF:base + PEER_QDIM])
        rank0, tops0 = _top16(s0)
        rank1, tops1 = _top16(s1)

        cand = jnp.concatenate([tops0[r0] + tops1[r1] for r0, r1 in _CELLS]
                               + [jnp.full((_CELL_ROWS - len(_CELLS), tn), NEG_INF, F32)], axis=0)
        work = cand
        sel = jnp.zeros(cand.shape, jnp.bool_)
        for _ in range(PEER_TOPK):
            m = jnp.max(work, axis=0, keepdims=True)
            idx = jnp.min(jnp.where(work == m, pos, 1 << 20), axis=0, keepdims=True)
            hit = pos == idx
            sel = jnp.logical_or(sel, hit)
            work = jnp.where(hit, NEG_INF, work)
        cmax = tops0[0] + tops1[0]
        e_sel = jnp.where(sel, jnp.exp(cand - cmax), 0.0)
        z = jnp.sum(e_sel, axis=0, keepdims=True)
        self_f = sel.astype(F32)
        counts = []
        start = 0
        for r0 in range(PEER_TOPK):
            n_cells = PEER_TOPK // (r0 + 1)
            counts.append(jnp.sum(self_f[start:start + n_cells, :], axis=0, keepdims=True))
            start += n_cells

        cnt = jnp.zeros(s0.shape, F32)
        for r0 in range(PEER_TOPK):
            cnt = jnp.where(rank0 == float(r0), counts[r0], cnt)
        a_ref[h] = jnp.exp(s0 - tops0[0]) / z
        l_ref[h] = cnt
        b_ref[h] = jnp.exp(s1 - tops1[0])
        c_ref[h] = rank1


def _peer_select(q, sub_keys, tn=256):
    n = q.shape[0]
    out = jax.ShapeDtypeStruct((PEER_HEADS, PEER_KEYS, n), F32)
    ospec = pl.BlockSpec((PEER_HEADS, PEER_KEYS, tn), lambda i: (0, 0, i))
    return pl.pallas_call(
        _select_kernel,
        grid=(n // tn,),
        in_specs=[pl.BlockSpec((tn, q.shape[1]), lambda i: (i, 0)),
                  pl.BlockSpec(sub_keys.shape, lambda i: (0, 0, 0, 0))],
        out_specs=[ospec] * 4,
        out_shape=[out] * 4,
        compiler_params=pltpu.CompilerParams(dimension_semantics=("parallel",), vmem_limit_bytes=VMEM_LIMIT),
        name="peer_select",
    )(q, sub_keys)


def _gelu(x):
    return 0.5 * x * (1.0 + lax.erf(x * (1.0 / math.sqrt(2.0))))


def _dense_kernel(tokt_ref, u_ref, vt_ref, a_ref, l_ref, b_ref, c_ref, x1_ref, fg_ref, out_ref, acc_ref, m_ref):
    e = pl.program_id(1)
    te = u_ref.shape[0]
    rows_per_step = te // PEER_KEYS

    @pl.when(e == 0)
    def _():
        acc_ref[...] = jnp.zeros_like(acc_ref)

    ht = _dot(u_ref[...], tokt_ref[...])
    for il in range(rows_per_step):
        i = e * rows_per_step + il
        gate = jnp.zeros((PEER_KEYS, ht.shape[1]), F32)
        for h in range(PEER_HEADS):
            a_row = a_ref[h, pl.ds(i, 1), :]
            l_row = l_ref[h, pl.ds(i, 1), :]
            gate = gate + jnp.where(c_ref[h] < l_row, b_ref[h], 0.0) * a_row
        act = _gelu(ht[il * PEER_KEYS:(il + 1) * PEER_KEYS, :]) * gate
        m_ref[il * PEER_KEYS:(il + 1) * PEER_KEYS, :] = act.astype(BF16)
    acc_ref[...] += _dot(vt_ref[...], m_ref[...])

    @pl.when(e == pl.num_programs(1) - 1)
    def _():
        x2 = x1_ref[...] + acc_ref[...].T
        out_ref[...] = x2 * lax.rsqrt(jnp.mean(x2 * x2, axis=-1, keepdims=True) + RMS_EPS) * fg_ref[...]


def _peer_dense(tok_t, u_b, v_t, a, l, b, c, x1, final_g, tn=512, te=512):
    n = x1.shape[0]
    sel_spec = pl.BlockSpec((PEER_HEADS, PEER_KEYS, tn), lambda t, e: (0, 0, t))
    return pl.pallas_call(
        _dense_kernel,
        grid=(n // tn, PEER_EXPERTS // te),
        in_specs=[pl.BlockSpec((D_MODEL, tn), lambda t, e: (0, t)),
                  pl.BlockSpec((te, D_MODEL), lambda t, e: (e, 0)),
                  pl.BlockSpec((D_MODEL, te), lambda t, e: (0, e)),
                  sel_spec, sel_spec, sel_spec, sel_spec,
                  pl.BlockSpec((tn, D_MODEL), lambda t, e: (t, 0)),
                  pl.BlockSpec((1, D_MODEL), lambda t, e: (0, 0))],
        out_specs=pl.BlockSpec((tn, D_MODEL), lambda t, e: (t, 0)),
        out_shape=jax.ShapeDtypeStruct((n, D_MODEL), F32),
        scratch_shapes=[pltpu.VMEM((D_MODEL, tn), F32), pltpu.VMEM((te, tn), BF16)],
        compiler_params=pltpu.CompilerParams(dimension_semantics=("parallel", "arbitrary"),
                                             vmem_limit_bytes=VMEM_LIMIT),
        name="peer_dense",
    )(tok_t, u_b, v_t, a, l, b, c, x1, final_g)


def _row(v):
    return v.reshape(1, -1).astype(F32)


def kernel(x, norm1_g, w_in, rw_mu, rw_w0, rw_w_up, rw_a0, rw_a_up, rw_g_up, rw_k_k, rw_k_a, rw_r_k, rw_gn_w, rw_gn_b, ml_conv_q_w, ml_conv_q_b, ml_conv_k_w, ml_conv_k_b, ml_b_i, ml_b_f, ml_norm_w, gate_b, p_rw, p_ml, w_out, norm2_g, peer_w_q, peer_sub_keys, peer_u, peer_v, final_g):
    bsz, t_len, d = x.shape
    n = bsz * t_len
    x2 = x.reshape(n, d)
    l = 0

    w = w_in[l]
    ml0 = RW_COLS
    if0 = RW_COLS + ML_MAIN
    gt0 = if0 + 2 * ML_HEADS
    wrw = w[:, :ml0].astype(BF16)
    wml = w[:, ml0:if0].astype(BF16)
    wif = jnp.pad(w[:, if0:gt0], ((0, 0), (0, LANES - 2 * ML_HEADS))).astype(BF16)
    wgt = w[:, gt0:].astype(BF16)
    z_rw, z_ml, z_if, z_gate = _inproj(x2, _row(norm1_g[l]), wrw, wml, wif, wgt)

    zero_lora = jnp.zeros((RW_DECAY_LORA, RW_WIDTH), F32)
    wup = jnp.concatenate([rw_w_up[l], zero_lora], axis=0)
    aup = jnp.concatenate([zero_lora, rw_a_up[l]], axis=0)
    y_rw = _rwkv(z_rw.reshape(bsz, t_len, RW_COLS), _row(rw_mu[l]), _row(rw_w0[l]), wup, _row(rw_a0[l]), aup,
                 rw_g_up[l], _row(rw_k_k[l]), _row(rw_k_a[l]), _row(rw_r_k[l]), _row(rw_gn_w[l]), _row(rw_gn_b[l]))

    conv_w = jnp.concatenate([ml_conv_q_w[l], ml_conv_k_w[l]], axis=1)
    conv_b = jnp.concatenate([ml_conv_q_b[l], ml_conv_k_b[l]]).reshape(1, -1)
    b_if = jnp.pad(jnp.concatenate([ml_b_i[l], ml_b_f[l]]), (0, LANES - 2 * ML_HEADS)).reshape(1, -1)
    y_ml = _mlstm(z_ml.reshape(bsz, t_len, ML_MAIN), z_if.reshape(bsz, t_len, LANES), conv_w, conv_b, b_if,
                  _row(ml_norm_w[l]))

    x1, tok, q = _merge(x2, y_rw.reshape(n, RW_WIDTH), y_ml.reshape(n, ML_WIDTH), z_gate, _row(gate_b[l]),
                        p_rw[l].astype(BF16), p_ml[l].astype(BF16), w_out[l].astype(BF16), _row(norm2_g[l]),
                        peer_w_q[l].astype(BF16))

    a, cnt, b, c = _peer_select(q, peer_sub_keys[l])
    out = _peer_dense(tok.T, peer_u[l].astype(BF16), peer_v[l].T.astype(BF16), a, cnt, b, c, x1, _row(final_g))
    return out.reshape(bsz, t_len, d)
```

```python
import functools
import math

import jax
import jax.numpy as jnp
from jax import lax
from jax.experimental import pallas as pl
from jax.experimental.pallas import tpu as pltpu

F32 = jnp.float32
BF16 = jnp.bfloat16

D_MODEL = 1024
CHUNK = 64
RMS_EPS = 1e-6
RW_HEADS = 8
RW_HEAD_DIM = 64
RW_WIDTH = 512
RW_DECAY_LORA = 64
RW_AAA_LORA = 64
RW_GATE_LORA = 128
RW_GN_EPS = 64e-5
RW_COLS = 3 * RW_WIDTH + RW_DECAY_LORA + RW_AAA_LORA + RW_GATE_LORA
RW_PAIRS = RW_HEADS // 2
ML_HEADS = 4
ML_HEAD_DIM = 128
ML_WIDTH = 512
ML_CONV = 4
ML_NORM_EPS = 1e-5
ML_MAIN = 4 * ML_WIDTH
PEER_HEADS = 8
PEER_KEYS = 128
PEER_EXPERTS = PEER_KEYS * PEER_KEYS
PEER_QDIM = 256
PEER_HALF = 128
PEER_TOPK = 16

LANES = 128
VMEM_LIMIT = 56 * 1024 * 1024

NEG_INF = float("-inf")


def _dot(a, b):
    return jnp.dot(a, b, preferred_element_type=F32)


def _dot_nt(a, b):
    return lax.dot_general(a, b, (((1,), (1,)), ((), ())), preferred_element_type=F32)


def _dot_tn(a, b):
    return lax.dot_general(a, b, (((0,), (0,)), ((), ())), preferred_element_type=F32)


def _sigmoid(x):
    return 1.0 / (1.0 + jnp.exp(-x))


def _softplus(x):
    return jnp.maximum(x, 0.0) + jnp.log(1.0 + jnp.exp(-jnp.abs(x)))


def _iota(shape, dim):
    return lax.broadcasted_iota(jnp.int32, shape, dim)


def _split2(x):
    hi = x.astype(BF16)
    return hi, (x - hi.astype(F32)).astype(BF16)


def _dot_exact_rhs(a, b_bf16):
    hi, lo = _split2(a)
    return _dot(hi, b_bf16) + _dot(lo, b_bf16)


def _dot_exact_lhs(a_bf16, b):
    hi = b.astype(BF16)
    rem = b - hi.astype(F32)
    mid = rem.astype(BF16)
    lo = (rem - mid.astype(F32)).astype(BF16)
    return _dot(a_bf16, hi) + _dot(a_bf16, mid) + _dot(a_bf16, lo)


def _inproj_kernel(x_ref, g_ref, wrw_ref, wml_ref, wif_ref, wgt_ref, zrw_ref, zml_ref, zif_ref, zgt_ref):
    x = x_ref[...]
    h = x * lax.rsqrt(jnp.mean(x * x, axis=-1, keepdims=True) + RMS_EPS) * g_ref[...]
    hb = h.astype(BF16)
    zrw_ref[...] = _dot(hb, wrw_ref[...])
    zml_ref[...] = _dot(hb, wml_ref[...])
    zif_ref[...] = _dot(hb, wif_ref[...])
    zgt_ref[...] = _dot(hb, wgt_ref[...])


def _inproj(x2, g, wrw, wml, wif, wgt, tm=256):
    n = x2.shape[0]
    row = lambda w: pl.BlockSpec((tm, w), lambda i: (i, 0))
    full = lambda a: pl.BlockSpec(a.shape, lambda i: (0,) * a.ndim)
    return pl.pallas_call(
        _inproj_kernel,
        grid=(n // tm,),
        in_specs=[row(D_MODEL), full(g), full(wrw), full(wml), full(wif), full(wgt)],
        out_specs=[row(wrw.shape[1]), row(wml.shape[1]), row(wif.shape[1]), row(wgt.shape[1])],
        out_shape=[jax.ShapeDtypeStruct((n, w.shape[1]), F32) for w in (wrw, wml, wif, wgt)],
        compiler_params=pltpu.CompilerParams(dimension_semantics=("parallel",), vmem_limit_bytes=VMEM_LIMIT),
        name="inproj",
    )(x2, g, wrw, wml, wif, wgt)


def _bd(xw):
    left = _iota(xw.shape, 1) < RW_HEAD_DIM
    return jnp.concatenate([jnp.where(left, xw, 0.0), jnp.where(left, 0.0, xw)], axis=0)


def _mm(xw, yw):
    return _dot(xw.astype(BF16), _bd(yw).astype(BF16))


def _rwkv_kernel(z_ref, mu_ref, w0_ref, wup_ref, a0_ref, aup_ref, gup_ref, kk_ref, ka_ref, rk_ref,
                 gnw_ref, gnb_ref, y_ref, prev_ref, st_ref):
    c = pl.program_id(1)
    nb = z_ref.shape[0]

    @pl.when(c == 0)
    def _():
        prev_ref[...] = jnp.zeros_like(prev_ref)
        st_ref[...] = jnp.zeros_like(st_ref)

    ii = _iota((CHUNK, LANES), 0)
    jj = _iota((CHUNK, LANES), 1) % RW_HEAD_DIM
    strict = jj < ii
    incl = jj <= ii
    eye_w = (jj == ii).astype(F32)
    same16 = (ii // 16) == (jj // 16)
    same32 = (ii // 32) == (jj // 32)
    tri = (_iota((CHUNK, CHUNK), 1) <= _iota((CHUNK, CHUNK), 0)).astype(BF16)
    r2 = _iota((LANES, LANES), 0)
    c2 = _iota((LANES, LANES), 1)
    blk = (r2 // RW_HEAD_DIM) == (c2 // RW_HEAD_DIM)
    ones_bd = blk.astype(BF16)
    eye2 = r2 == c2
    first = _iota((CHUNK, RW_COLS), 0) == 0
    o1, o2, o3 = RW_WIDTH, 2 * RW_WIDTH, 3 * RW_WIDTH

    items = []
    for b in range(nb):
        z = z_ref[b]
        z_prev = jnp.where(first, jnp.broadcast_to(prev_ref[b, 0:1, :], z.shape), pltpu.roll(z, 1, 0))
        prev_ref[b, 0:1, :] = z[CHUNK - 1:CHUNK, :]
        zs = z + (z_prev - z) * mu_ref[...]
        r = zs[:, :o1]
        k = zs[:, o1:o2]
        v = zs[:, o2:o3]
        zwa = zs[:, o3:o3 + LANES]
        zg = zs[:, o3 + LANES:]
        w_log = -_softplus(-(w0_ref[...] + _dot(jnp.tanh(zwa).astype(BF16), wup_ref[...]))) - 0.5
        logd = -jnp.exp(w_log)
        a = _sigmoid(a0_ref[...] + _dot(zwa.astype(BF16), aup_ref[...]))
        g = _dot(_sigmoid(zg).astype(BF16), gup_ref[...])
        kk_raw = k * kk_ref[...]
        k = k * (1.0 + (a - 1.0) * ka_ref[...])
        rkr = r * k * rk_ref[...]
        for p in range(RW_PAIRS):
            sl = slice(p * LANES, (p + 1) * LANES)
            items.append(dict(b=b, p=p, sl=sl, r=r[:, sl], k=k[:, sl], v=v[:, sl], a=a[:, sl], ld=logd[:, sl],
                              kk=kk_raw[:, sl], rkr=rkr[:, sl], g=g[:, sl]))

    def each(fn):
        return [fn(it) for it in items]

    def put(name, vals):
        for it, val in zip(items, vals):
            it[name] = val

    put("ss", each(lambda it: _dot_exact_rhs(it["kk"] * it["kk"], ones_bd)))
    put("lcum", each(lambda it: _dot_exact_lhs(tri, it["ld"])))
    put("bonus", each(lambda it: _dot_exact_rhs(it["rkr"], ones_bd)))
    for it in items:
        kk_n = it["kk"] / jnp.maximum(jnp.sqrt(it["ss"]), 1e-12)
        lcum = it["lcum"]
        l_end = lcum[CHUNK - 1:CHUNK, :]
        p_inv = jnp.exp(-lcum)
        to_end = jnp.exp(l_end - lcum)
        beta = kk_n * it["a"]
        it["l_end"] = l_end
        it["rt"] = it["r"] * jnp.exp(lcum)
        it["kt"] = kk_n * jnp.exp(lcum - it["ld"])
        it["kh_end"] = it["k"] * to_end
        it["bh_end"] = beta * to_end
        it["lhs"] = jnp.concatenate([it["kt"], it["rt"]], axis=0).astype(BF16)
        it["rhs"] = jnp.concatenate([_bd(it["k"] * p_inv), _bd(beta * p_inv)], axis=0).astype(BF16)
    put("sc", each(lambda it: _dot_nt(it["lhs"], it["rhs"])))
    for it in items:
        sc = it["sc"]
        it["a_vk"] = jnp.where(strict, sc[:CHUNK, :LANES], 0.0)
        it["n_w"] = jnp.where(strict, sc[:CHUNK, LANES:], 0.0)
        it["a_rk"] = jnp.where(incl, sc[CHUNK:, :LANES], 0.0)
        it["a_rb"] = jnp.where(incl, sc[CHUNK:, LANES:], 0.0)
        it["d1"] = jnp.where(same16, it["n_w"], 0.0)

    put("av", each(lambda it: _mm(it["a_vk"], it["v"])))
    put("d2", each(lambda it: _mm(it["d1"], it["d1"])))
    put("d4", each(lambda it: _mm(it["d2"], it["d2"])))
    put("t", each(lambda it: _mm(eye_w - it["d1"], eye_w + it["d2"])))
    put("d8", each(lambda it: _mm(it["d4"], it["d4"])))
    put("t", each(lambda it: _mm(it["t"], eye_w + it["d4"])))
    put("t", each(lambda it: _mm(it["t"], eye_w + it["d8"])))
    off32_mask = jnp.logical_and(same32, jnp.logical_not(same16))
    put("tmp", each(lambda it: _mm(it["t"], jnp.where(off32_mask, it["n_w"], 0.0))))
    put("t", each(lambda it: it["t"] - _mm(it["tmp"], it["t"])))
    put("tmp", each(lambda it: _mm(it["t"], jnp.where(same32, 0.0, it["n_w"]))))
    put("t", each(lambda it: it["t"] - _mm(it["tmp"], it["t"])))

    put("x_w", each(lambda it: _mm(it["t"], it["av"])))
    put("w_w", each(lambda it: _mm(it["t"], it["kt"])))
    put("st", each(lambda it: st_ref[it["b"], it["p"]]))
    put("st_b", each(lambda it: it["st"].astype(BF16)))
    put("u", each(lambda it: -(_dot(it["w_w"].astype(BF16), it["st_b"]) + it["x_w"])))
    put("y", each(lambda it: _dot(it["rt"].astype(BF16), it["st_b"]) + _mm(it["a_rk"], it["v"])
                  + _mm(it["a_rb"], it["u"])))
    put("d_st", each(lambda it: _dot_tn(jnp.concatenate([it["kh_end"], it["bh_end"]], axis=0).astype(BF16),
                                        jnp.concatenate([it["v"], it["u"]], axis=0).astype(BF16))))
    for it in items:
        p_col = jnp.sum(jnp.where(eye2, jnp.broadcast_to(jnp.exp(it["l_end"]), (LANES, LANES)), 0.0),
                        axis=1, keepdims=True)
        st_ref[it["b"], it["p"]] = it["st"] * p_col + jnp.where(blk, it["d_st"], 0.0)

    put("mean", each(lambda it: _dot_exact_rhs(it["y"], ones_bd) * (1.0 / RW_HEAD_DIM)))
    put("yc", each(lambda it: it["y"] - it["mean"]))
    put("var", each(lambda it: _dot_exact_rhs(it["yc"] * it["yc"], ones_bd) * (1.0 / RW_HEAD_DIM)))
    for it in items:
        yn = it["yc"] * lax.rsqrt(it["var"] + RW_GN_EPS) * gnw_ref[:, it["sl"]] + gnb_ref[:, it["sl"]]
        y_ref[it["b"], :, it["sl"]] = (yn + it["bonus"] * it["v"]) * it["g"]


def _rwkv(z_rw, mu, w0, wup, a0, aup, gup, k_k, k_a, r_k, gn_w, gn_b, nb=2):
    bsz, t_len, _ = z_rw.shape
    full = lambda a: pl.BlockSpec(a.shape, lambda b, c: (0,) * a.ndim)
    params = (mu, w0, wup, a0, aup, gup, k_k, k_a, r_k, gn_w, gn_b)
    return pl.pallas_call(
        _rwkv_kernel,
        grid=(bsz // nb, t_len // CHUNK),
        in_specs=[pl.BlockSpec((nb, CHUNK, RW_COLS), lambda b, c: (b, c, 0))] + [full(a) for a in params],
        out_specs=pl.BlockSpec((nb, CHUNK, RW_WIDTH), lambda b, c: (b, c, 0)),
        out_shape=jax.ShapeDtypeStruct((bsz, t_len, RW_WIDTH), F32),
        scratch_shapes=[pltpu.VMEM((nb, 8, RW_COLS), F32), pltpu.VMEM((nb, RW_PAIRS, LANES, LANES), F32)],
        compiler_params=pltpu.CompilerParams(dimension_semantics=("parallel", "arbitrary"),
                                             vmem_limit_bytes=VMEM_LIMIT),
        name="rwkv7",
    )(z_rw, *params)


def _mlstm_kernel(z_ref, zif_ref, cw_ref, cb_ref, bif_ref, nw_ref, h_ref, conv_ref, c_ref, n_ref, m_ref):
    c = pl.program_id(1)

    @pl.when(c == 0)
    def _():
        conv_ref[0:8, :] = jnp.zeros((8, 2 * ML_WIDTH), F32)
        c_ref[...] = jnp.zeros_like(c_ref)
        n_ref[...] = jnp.zeros_like(n_ref)
        m_ref[...] = jnp.zeros_like(m_ref)

    conv_ref[8:8 + CHUNK, :] = z_ref[:, 0:2 * ML_WIDTH]
    acc = jnp.broadcast_to(cb_ref[...], (CHUNK, 2 * ML_WIDTH))
    for j in range(ML_CONV):
        off = 8 - (ML_CONV - 1) + j
        acc = acc + cw_ref[j:j + 1, :] * conv_ref[off:off + CHUNK, :]
    conv_ref[0:8, :] = conv_ref[CHUNK:CHUNK + 8, :]
    qk = acc * _sigmoid(acc)
    q = qk[:, :ML_WIDTH]
    k = qk[:, ML_WIDTH:] * (ML_HEAD_DIM ** -0.5)
    v = z_ref[:, 2 * ML_WIDTH:3 * ML_WIDTH]
    o_pre = z_ref[:, 3 * ML_WIDTH:4 * ML_WIDTH]

    gz = zif_ref[...] + bif_ref[...]
    lf = -_softplus(-gz)
    tri = (_iota((CHUNK, CHUNK), 1) <= _iota((CHUNK, CHUNK), 0))
    bcum = _dot_exact_lhs(tri.astype(BF16), lf)
    gz_t = gz.T
    bcum_t = bcum.T

    hs = []
    for h in range(ML_HEADS):
        sl = slice(h * ML_HEAD_DIM, (h + 1) * ML_HEAD_DIM)
        q_h, k_h, v_h = q[:, sl], k[:, sl], v[:, sl]
        fh = ML_HEADS + h
        bc_col = bcum[:, fh:fh + 1]
        bc_row = bcum_t[fh:fh + 1, :]
        ig_col = gz[:, h:h + 1]
        ig_row = gz_t[h:h + 1, :]
        g_end = bcum[CHUNK - 1:CHUNK, fh:fh + 1]
        d_log = jnp.where(tri, bc_col - bc_row + ig_row, NEG_INF)
        a_row = g_end - bc_row + ig_row
        m_loc = jnp.max(a_row, axis=1, keepdims=True)
        wts = jnp.exp(g_end - bc_col + ig_col - m_loc)
        k_w = k_h * wts
        c_loc = _dot_tn(v_h.astype(BF16), k_w.astype(BF16))
        n_loc = jnp.sum(k_w, axis=0, keepdims=True)

        c_in = c_ref[h]
        n_in = n_ref[h, 0:1, :]
        m_in = m_ref[h, 0:1, 0:1]
        inter_log = bc_col + m_in
        m_t = jnp.maximum(jnp.max(d_log, axis=1, keepdims=True), inter_log)
        scores = _dot_nt(q_h.astype(BF16), k_h.astype(BF16)) * jnp.exp(d_log - m_t)
        inter_w = jnp.exp(inter_log - m_t)
        num = _dot(scores.astype(BF16), v_h.astype(BF16)) \
            + inter_w * _dot_nt(q_h.astype(BF16), c_in.astype(BF16))
        den = jnp.sum(scores, axis=1, keepdims=True) + inter_w * jnp.sum(q_h * n_in, axis=1, keepdims=True)
        hh = num / jnp.maximum(jnp.abs(den), jnp.exp(-m_t))
        mean = jnp.mean(hh, axis=1, keepdims=True)
        hc = hh - mean
        var = jnp.mean(hc * hc, axis=1, keepdims=True)
        hn = hc * lax.rsqrt(var + ML_NORM_EPS) * nw_ref[:, sl]
        hs.append(hn * _sigmoid(o_pre[:, sl]))

        m_new = jnp.maximum(g_end + m_in, m_loc)
        s_old = jnp.exp(g_end + m_in - m_new)
        s_new = jnp.exp(m_loc - m_new)
        c_ref[h] = s_old * c_in + s_new * c_loc
        n_ref[h] = jnp.broadcast_to(s_old * n_in + s_new * n_loc, (8, ML_HEAD_DIM))
        m_ref[h] = jnp.broadcast_to(m_new, (8, LANES))

    h_ref[...] = jnp.concatenate(hs, axis=1)


def _mlstm(z_ml, z_if, conv_w, conv_b, b_if, norm_w):
    bsz, t_len, _ = z_ml.shape
    full = lambda a: pl.BlockSpec(a.shape, lambda b, c: (0,) * a.ndim)
    params = (conv_w, conv_b, b_if, norm_w)
    return pl.pallas_call(
        _mlstm_kernel,
        grid=(bsz, t_len // CHUNK),
        in_specs=[pl.BlockSpec((None, CHUNK, ML_MAIN), lambda b, c: (b, c, 0)),
                  pl.BlockSpec((None, CHUNK, LANES), lambda b, c: (b, c, 0))] + [full(a) for a in params],
        out_specs=pl.BlockSpec((None, CHUNK, ML_WIDTH), lambda b, c: (b, c, 0)),
        out_shape=jax.ShapeDtypeStruct((bsz, t_len, ML_WIDTH), F32),
        scratch_shapes=[pltpu.VMEM((8 + CHUNK, 2 * ML_WIDTH), F32),
                        pltpu.VMEM((ML_HEADS, ML_HEAD_DIM, ML_HEAD_DIM), F32),
                        pltpu.VMEM((ML_HEADS, 8, ML_HEAD_DIM), F32),
                        pltpu.VMEM((ML_HEADS, 8, LANES), F32)],
        compiler_params=pltpu.CompilerParams(dimension_semantics=("parallel", "arbitrary"),
                                             vmem_limit_bytes=VMEM_LIMIT),
        name="mlstm",
    )(z_ml, z_if, *params)


def _merge_kernel(x_ref, yrw_ref, yml_ref, zg_ref, gb_ref, prw_ref, pml_ref, wout_ref, n2_ref, wq_ref,
                  x1_ref, tok_ref, q_ref):
    gate = _sigmoid(zg_ref[...] + gb_ref[...])
    m_rw = _dot(yrw_ref[...].astype(BF16), prw_ref[...])
    m_ml = _dot(yml_ref[...].astype(BF16), pml_ref[...])
    merged = gate[:, :D_MODEL] * m_rw + gate[:, D_MODEL:] * m_ml
    x1 = x_ref[...] + _dot(merged.astype(BF16), wout_ref[...])
    x1_ref[...] = x1
    tok = x1 * lax.rsqrt(jnp.mean(x1 * x1, axis=-1, keepdims=True) + RMS_EPS) * n2_ref[...]
    tok_b = tok.astype(BF16)
    tok_ref[...] = tok_b
    q_ref[...] = _dot(tok_b, wq_ref[...])


def _merge(x2, y_rw, y_ml, z_gate, gate_b, p_rw, p_ml, w_out, n2, w_q, tm=256):
    n = x2.shape[0]
    row = lambda w: pl.BlockSpec((tm, w), lambda i: (i, 0))
    full = lambda a: pl.BlockSpec(a.shape, lambda i: (0,) * a.ndim)
    qd = w_q.shape[1]
    return pl.pallas_call(
        _merge_kernel,
        grid=(n // tm,),
        in_specs=[row(D_MODEL), row(RW_WIDTH), row(ML_WIDTH), row(2 * D_MODEL), full(gate_b), full(p_rw),
                  full(p_ml), full(w_out), full(n2), full(w_q)],
        out_specs=[row(D_MODEL), row(D_MODEL), row(qd)],
        out_shape=[jax.ShapeDtypeStruct((n, D_MODEL), F32), jax.ShapeDtypeStruct((n, D_MODEL), BF16),
                   jax.ShapeDtypeStruct((n, qd), F32)],
        compiler_params=pltpu.CompilerParams(dimension_semantics=("parallel",), vmem_limit_bytes=VMEM_LIMIT),
        name="merge",
    )(x2, y_rw, y_ml, z_gate, gate_b, p_rw, p_ml, w_out, n2, w_q)


_CELLS = [(r0, r1) for r0 in range(PEER_TOPK) for r1 in range(PEER_TOPK) if (r0 + 1) * (r1 + 1) <= PEER_TOPK]
_CELL_ROWS = 56


def _top16_exact(s):
    kiota = _iota(s.shape, 0)
    rank = jnp.full(s.shape, float(PEER_KEYS - 1), F32)
    tops = []
    for r in range(PEER_TOPK):
        m = jnp.max(s, axis=0, keepdims=True)
        idx = jnp.min(jnp.where(s == m, kiota, PEER_KEYS), axis=0, keepdims=True)
        hit = kiota == idx
        rank = jnp.where(hit, float(r), rank)
        s = jnp.where(hit, NEG_INF, s)
        tops.append(m)
    return rank, jnp.concatenate(tops, axis=0)


def _top16_distinct(s):
    rank = jnp.full(s.shape, float(PEER_KEYS - 1), F32)
    tops = []
    for r in range(PEER_TOPK):
        m = jnp.max(s, axis=0, keepdims=True)
        hit = s == m
        rank = jnp.where(hit, float(r), rank)
        s = jnp.where(hit, NEG_INF, s)
        tops.append(m)
    return rank, jnp.concatenate(tops, axis=0)


def _select_kernel(q_ref, sk_ref, a_ref, l_ref, b_ref, c_ref, s_ref, rank_ref, tops_ref):
    tn = q_ref.shape[0]
    pos_col = jnp.concatenate(
        [jnp.full((1, 1), r0 * PEER_TOPK + r1, jnp.int32) for r0, r1 in _CELLS]
        + [jnp.full((_CELL_ROWS - len(_CELLS), 1), 1 << 20, jnp.int32)], axis=0)
    pos = jnp.broadcast_to(pos_col, (_CELL_ROWS, tn))
    for h in range(PEER_HEADS):
        for half in range(2):
            base = h * PEER_QDIM + half * PEER_HALF
            s = _dot_nt(sk_ref[h, half], q_ref[:, base:base + PEER_HALF])
            s_ref[half] = s
            rank, tops = _top16_distinct(s)
            rank_ref[half] = rank
            tops_ref[half] = tops
            n_ranked = jnp.sum((rank < float(PEER_TOPK)).astype(F32), axis=0, keepdims=True)

            @pl.when(jnp.max(n_ranked) > float(PEER_TOPK))
            def _():
                rank_x, tops_x = _top16_exact(s_ref[half])
                rank_ref[half] = rank_x
                tops_ref[half] = tops_x

        tops0 = tops_ref[0]
        tops1 = tops_ref[1]
        cand = jnp.concatenate([tops0[r0:r0 + 1] + tops1[r1:r1 + 1] for r0, r1 in _CELLS]
                               + [jnp.full((_CELL_ROWS - len(_CELLS), tn), NEG_INF, F32)], axis=0)
        work = cand
        sel = jnp.zeros(cand.shape, jnp.bool_)
        for _ in range(PEER_TOPK):
            m = jnp.max(work, axis=0, keepdims=True)
            idx = jnp.min(jnp.where(work == m, pos, 1 << 20), axis=0, keepdims=True)
            hit = pos == idx
            sel = jnp.logical_or(sel, hit)
            work = jnp.where(hit, NEG_INF, work)
        cmax = tops0[0:1] + tops1[0:1]
        e_sel = jnp.where(sel, jnp.exp(cand - cmax), 0.0)
        z = jnp.sum(e_sel, axis=0, keepdims=True)
        self_f = sel.astype(F32)
        counts = []
        start = 0
        for r0 in range(PEER_TOPK):
            n_cells = PEER_TOPK // (r0 + 1)
            counts.append(jnp.sum(self_f[start:start + n_cells, :], axis=0, keepdims=True))
            start += n_cells

        rank0 = rank_ref[0]
        cnt = jnp.zeros(rank0.shape, F32)
        for r0 in range(PEER_TOPK):
            cnt = jnp.where(rank0 == float(r0), counts[r0], cnt)
        a_ref[h] = jnp.exp(s_ref[0] - tops0[0:1]) / z
        l_ref[h] = cnt
        b_ref[h] = jnp.exp(s_ref[1] - tops1[0:1]).astype(BF16)
        c_ref[h] = rank_ref[1].astype(BF16)


def _peer_select(q, sub_keys, tn=256):
    n = q.shape[0]
    shape = (PEER_HEADS, PEER_KEYS, n)
    ospec = pl.BlockSpec((PEER_HEADS, PEER_KEYS, tn), lambda i: (0, 0, i))
    return pl.pallas_call(
        _select_kernel,
        grid=(n // tn,),
        in_specs=[pl.BlockSpec((tn, q.shape[1]), lambda i: (i, 0)),
                  pl.BlockSpec(sub_keys.shape, lambda i: (0, 0, 0, 0))],
        out_specs=[ospec] * 4,
        out_shape=[jax.ShapeDtypeStruct(shape, F32), jax.ShapeDtypeStruct(shape, F32),
                   jax.ShapeDtypeStruct(shape, BF16), jax.ShapeDtypeStruct(shape, BF16)],
        scratch_shapes=[pltpu.VMEM((2, PEER_KEYS, tn), F32), pltpu.VMEM((2, PEER_KEYS, tn), F32),
                        pltpu.VMEM((2, PEER_TOPK, tn), F32)],
        compiler_params=pltpu.CompilerParams(dimension_semantics=("parallel",), vmem_limit_bytes=VMEM_LIMIT),
        name="peer_select",
    )(q, sub_keys)


GATE_COLS = 256


def _gelu(x):
    return 0.5 * x * (1.0 + lax.erf(x * (1.0 / math.sqrt(2.0))))


def _gate_stage(i0, a_ref, l_ref, b_ref, c_ref, h_ref, m_ref):
    te, tn = h_ref.shape
    zero = jnp.zeros((PEER_KEYS, GATE_COLS), BF16)
    for il in range(te // PEER_KEYS):
        i = i0 + il
        rows = slice(il * PEER_KEYS, (il + 1) * PEER_KEYS)
        for cg in range(tn // GATE_COLS):
            cols = slice(cg * GATE_COLS, (cg + 1) * GATE_COLS)
            gate = zero
            for h in range(PEER_HEADS):
                a_b = jnp.broadcast_to(a_ref[h, pl.ds(i, 1), cols].astype(BF16), zero.shape)
                l_b = jnp.broadcast_to(l_ref[h, pl.ds(i, 1), cols].astype(BF16), zero.shape)
                gate = gate + jnp.where(c_ref[h, :, cols] < l_b, b_ref[h, :, cols], zero) * a_b
            m_ref[rows, cols] = _gelu(h_ref[rows, cols]).astype(BF16) * gate


def _dense_kernel(tokt_ref, u_ref, vt_ref, a_ref, l_ref, b_ref, c_ref, x1_ref, fg_ref, out_ref,
                  acc_ref, h0_ref, h1_ref, m0_ref, m1_ref):
    e = pl.program_id(1)
    te = h0_ref.shape[0]
    n_tiles = PEER_EXPERTS // te
    rows_per_tile = te // PEER_KEYS

    @pl.when(e == 0)
    def _():
        acc_ref[...] = jnp.zeros_like(acc_ref)
        m0_ref[...] = jnp.zeros_like(m0_ref)
        h1_ref[...] = jnp.zeros_like(h1_ref)

    tok_t = tokt_ref[...]
    k_odd_prev = jnp.clip(2 * e - 1, 0, n_tiles - 1)
    k_even = jnp.minimum(2 * e, n_tiles - 1)

    acc_ref[...] += _dot(vt_ref[:, :te], m0_ref[...])
    h0_ref[...] = _dot(u_ref[:te, :], tok_t)
    _gate_stage(k_odd_prev * rows_per_tile, a_ref, l_ref, b_ref, c_ref, h1_ref, m1_ref)

    acc_ref[...] += _dot(vt_ref[:, te:], m1_ref[...])
    h1_ref[...] = _dot(u_ref[te:, :], tok_t)
    _gate_stage(k_even * rows_per_tile, a_ref, l_ref, b_ref, c_ref, h0_ref, m0_ref)

    @pl.when(e == pl.num_programs(1) - 1)
    def _():
        x2 = x1_ref[...] + acc_ref[...].T
        out_ref[...] = x2 * lax.rsqrt(jnp.mean(x2 * x2, axis=-1, keepdims=True) + RMS_EPS) * fg_ref[...]


def _peer_dense(tok_t, u_b, v_t, a, l, b, c, x1, final_g, tn=512, te=512):
    n = x1.shape[0]
    n_steps = PEER_EXPERTS // (2 * te) + 1
    sel_spec = pl.BlockSpec((PEER_HEADS, PEER_KEYS, tn), lambda t, e: (0, 0, t))
    return pl.pallas_call(
        _dense_kernel,
        grid=(n // tn, n_steps),
        in_specs=[pl.BlockSpec((D_MODEL, tn), lambda t, e: (0, t)),
                  pl.BlockSpec((2 * te, D_MODEL), lambda t, e: (jnp.minimum(e, n_steps - 2), 0)),
                  pl.BlockSpec((D_MODEL, 2 * te), lambda t, e: (0, jnp.maximum(e - 1, 0))),
                  sel_spec, sel_spec, sel_spec, sel_spec,
                  pl.BlockSpec((tn, D_MODEL), lambda t, e: (t, 0)),
                  pl.BlockSpec((1, D_MODEL), lambda t, e: (0, 0))],
        out_specs=pl.BlockSpec((tn, D_MODEL), lambda t, e: (t, 0)),
        out_shape=jax.ShapeDtypeStruct((n, D_MODEL), F32),
        scratch_shapes=[pltpu.VMEM((D_MODEL, tn), F32), pltpu.VMEM((te, tn), F32), pltpu.VMEM((te, tn), F32),
                        pltpu.VMEM((te, tn), BF16), pltpu.VMEM((te, tn), BF16)],
        compiler_params=pltpu.CompilerParams(dimension_semantics=("parallel", "arbitrary"),
                                             vmem_limit_bytes=VMEM_LIMIT),
        name="peer_dense",
    )(tok_t, u_b, v_t, a, l, b, c, x1, final_g)


def _row(v):
    return v.reshape(1, -1).astype(F32)


def kernel(x, norm1_g, w_in, rw_mu, rw_w0, rw_w_up, rw_a0, rw_a_up, rw_g_up, rw_k_k, rw_k_a, rw_r_k, rw_gn_w, rw_gn_b, ml_conv_q_w, ml_conv_q_b, ml_conv_k_w, ml_conv_k_b, ml_b_i, ml_b_f, ml_norm_w, gate_b, p_rw, p_ml, w_out, norm2_g, peer_w_q, peer_sub_keys, peer_u, peer_v, final_g):
    bsz, t_len, d = x.shape
    n = bsz * t_len
    x2 = x.reshape(n, d)
    l = 0

    w = w_in[l]
    ml0 = RW_COLS
    if0 = RW_COLS + ML_MAIN
    gt0 = if0 + 2 * ML_HEADS
    wrw = w[:, :ml0].astype(BF16)
    wml = w[:, ml0:if0].astype(BF16)
    wif = jnp.pad(w[:, if0:gt0], ((0, 0), (0, LANES - 2 * ML_HEADS))).astype(BF16)
    wgt = w[:, gt0:].astype(BF16)
    z_rw, z_ml, z_if, z_gate = _inproj(x2, _row(norm1_g[l]), wrw, wml, wif, wgt)

    zero_lora = jnp.zeros((RW_DECAY_LORA, RW_WIDTH), F32)
    wup = jnp.concatenate([rw_w_up[l], zero_lora], axis=0).astype(BF16)
    aup = jnp.concatenate([zero_lora, rw_a_up[l]], axis=0).astype(BF16)
    y_rw = _rwkv(z_rw.reshape(bsz, t_len, RW_COLS), _row(rw_mu[l]), _row(rw_w0[l]), wup, _row(rw_a0[l]), aup,
                 rw_g_up[l].astype(BF16), _row(rw_k_k[l]), _row(rw_k_a[l]), _row(rw_r_k[l]), _row(rw_gn_w[l]),
                 _row(rw_gn_b[l]))

    conv_w = jnp.concatenate([ml_conv_q_w[l], ml_conv_k_w[l]], axis=1)
    conv_b = jnp.concatenate([ml_conv_q_b[l], ml_conv_k_b[l]]).reshape(1, -1)
    b_if = jnp.pad(jnp.concatenate([ml_b_i[l], ml_b_f[l]]), (0, LANES - 2 * ML_HEADS)).reshape(1, -1)
    y_ml = _mlstm(z_ml.reshape(bsz, t_len, ML_MAIN), z_if.reshape(bsz, t_len, LANES), conv_w, conv_b, b_if,
                  _row(ml_norm_w[l]))

    x1, tok, q = _merge(x2, y_rw.reshape(n, RW_WIDTH), y_ml.reshape(n, ML_WIDTH), z_gate, _row(gate_b[l]),
                        p_rw[l].astype(BF16), p_ml[l].astype(BF16), w_out[l].astype(BF16), _row(norm2_g[l]),
                        peer_w_q[l].astype(BF16))

    a, cnt, b, c = _peer_select(q, peer_sub_keys[l])
    out = _peer_dense(tok.T, peer_u[l].astype(BF16), peer_v[l].T.astype(BF16), a, cnt, b, c, x1, _row(final_g))
    return out.reshape(bsz, t_len, d)
```

```python
import functools
import math

import jax
import jax.numpy as jnp
from jax import lax
from jax.experimental import pallas as pl
from jax.experimental.pallas import tpu as pltpu

F32 = jnp.float32
BF16 = jnp.bfloat16

D_MODEL = 1024
CHUNK = 64
RMS_EPS = 1e-6
RW_HEADS = 8
RW_HEAD_DIM = 64
RW_WIDTH = 512
RW_DECAY_LORA = 64
RW_AAA_LORA = 64
RW_GATE_LORA = 128
RW_GN_EPS = 64e-5
RW_COLS = 3 * RW_WIDTH + RW_DECAY_LORA + RW_AAA_LORA + RW_GATE_LORA
RW_PAIRS = RW_HEADS // 2
ML_HEADS = 4
ML_HEAD_DIM = 128
ML_WIDTH = 512
ML_CONV = 4
ML_NORM_EPS = 1e-5
ML_MAIN = 4 * ML_WIDTH
PEER_HEADS = 8
PEER_KEYS = 128
PEER_EXPERTS = PEER_KEYS * PEER_KEYS
PEER_QDIM = 256
PEER_HALF = 128
PEER_TOPK = 16

LANES = 128
VMEM_LIMIT = 56 * 1024 * 1024

NEG_INF = float("-inf")


def _dot(a, b):
    return jnp.dot(a, b, preferred_element_type=F32)


def _dot_nt(a, b):
    return lax.dot_general(a, b, (((1,), (1,)), ((), ())), preferred_element_type=F32)


def _dot_tn(a, b):
    return lax.dot_general(a, b, (((0,), (0,)), ((), ())), preferred_element_type=F32)


def _sigmoid(x):
    return 1.0 / (1.0 + jnp.exp(-x))


def _softplus(x):
    return jnp.maximum(x, 0.0) + jnp.log(1.0 + jnp.exp(-jnp.abs(x)))


def _iota(shape, dim):
    return lax.broadcasted_iota(jnp.int32, shape, dim)


def _split2(x):
    hi = x.astype(BF16)
    return hi, (x - hi.astype(F32)).astype(BF16)


def _dot_exact_rhs(a, b_bf16):
    hi, lo = _split2(a)
    return _dot(hi, b_bf16) + _dot(lo, b_bf16)


def _dot_exact_lhs(a_bf16, b):
    hi = b.astype(BF16)
    rem = b - hi.astype(F32)
    mid = rem.astype(BF16)
    lo = (rem - mid.astype(F32)).astype(BF16)
    return _dot(a_bf16, hi) + _dot(a_bf16, mid) + _dot(a_bf16, lo)


def _inproj_kernel(x_ref, g_ref, wrw_ref, wml_ref, wif_ref, wgt_ref, zrw_ref, zml_ref, zif_ref, zgt_ref):
    x = x_ref[...]
    h = x * lax.rsqrt(jnp.mean(x * x, axis=-1, keepdims=True) + RMS_EPS) * g_ref[...]
    hb = h.astype(BF16)
    zrw_ref[...] = _dot(hb, wrw_ref[...])
    zml_ref[...] = _dot(hb, wml_ref[...])
    zif_ref[...] = _dot(hb, wif_ref[...])
    zgt_ref[...] = _dot(hb, wgt_ref[...])


def _inproj(x2, g, wrw, wml, wif, wgt, tm=256):
    n = x2.shape[0]
    row = lambda w: pl.BlockSpec((tm, w), lambda i: (i, 0))
    full = lambda a: pl.BlockSpec(a.shape, lambda i: (0,) * a.ndim)
    return pl.pallas_call(
        _inproj_kernel,
        grid=(n // tm,),
        in_specs=[row(D_MODEL), full(g), full(wrw), full(wml), full(wif), full(wgt)],
        out_specs=[row(wrw.shape[1]), row(wml.shape[1]), row(wif.shape[1]), row(wgt.shape[1])],
        out_shape=[jax.ShapeDtypeStruct((n, w.shape[1]), F32) for w in (wrw, wml, wif, wgt)],
        compiler_params=pltpu.CompilerParams(dimension_semantics=("parallel",), vmem_limit_bytes=VMEM_LIMIT),
        name="inproj",
    )(x2, g, wrw, wml, wif, wgt)


def _bd(xw):
    left = _iota(xw.shape, 1) < RW_HEAD_DIM
    return jnp.concatenate([jnp.where(left, xw, 0.0), jnp.where(left, 0.0, xw)], axis=0)


def _mm(xw, yw):
    return _dot(xw.astype(BF16), _bd(yw).astype(BF16))


def _rwkv_kernel(z_ref, mu_ref, w0_ref, wup_ref, a0_ref, aup_ref, gup_ref, kk_ref, ka_ref, rk_ref,
                 gnw_ref, gnb_ref, y_ref, prev_ref, st_ref):
    c = pl.program_id(1)
    nb = z_ref.shape[0]

    @pl.when(c == 0)
    def _():
        prev_ref[...] = jnp.zeros_like(prev_ref)
        st_ref[...] = jnp.zeros_like(st_ref)

    ii = _iota((CHUNK, LANES), 0)
    jj = _iota((CHUNK, LANES), 1) % RW_HEAD_DIM
    strict = jj < ii
    incl = jj <= ii
    eye_w = (jj == ii).astype(F32)
    same16 = (ii // 16) == (jj // 16)
    same32 = (ii // 32) == (jj // 32)
    tri = (_iota((CHUNK, CHUNK), 1) <= _iota((CHUNK, CHUNK), 0)).astype(BF16)
    r2 = _iota((LANES, LANES), 0)
    c2 = _iota((LANES, LANES), 1)
    blk = (r2 // RW_HEAD_DIM) == (c2 // RW_HEAD_DIM)
    ones_bd = blk.astype(BF16)
    eye2 = r2 == c2
    first = _iota((CHUNK, RW_COLS), 0) == 0
    o1, o2, o3 = RW_WIDTH, 2 * RW_WIDTH, 3 * RW_WIDTH

    items = []
    for b in range(nb):
        z = z_ref[b]
        z_prev = jnp.where(first, jnp.broadcast_to(prev_ref[b, 0:1, :], z.shape), pltpu.roll(z, 1, 0))
        prev_ref[b, 0:1, :] = z[CHUNK - 1:CHUNK, :]
        zs = z + (z_prev - z) * mu_ref[...]
        r = zs[:, :o1]
        k = zs[:, o1:o2]
        v = zs[:, o2:o3]
        zwa = zs[:, o3:o3 + LANES]
        zg = zs[:, o3 + LANES:]
        w_log = -_softplus(-(w0_ref[...] + _dot(jnp.tanh(zwa).astype(BF16), wup_ref[...]))) - 0.5
        logd = -jnp.exp(w_log)
        a = _sigmoid(a0_ref[...] + _dot(zwa.astype(BF16), aup_ref[...]))
        g = _dot(_sigmoid(zg).astype(BF16), gup_ref[...])
        kk_raw = k * kk_ref[...]
        k = k * (1.0 + (a - 1.0) * ka_ref[...])
        rkr = r * k * rk_ref[...]
        for p in range(RW_PAIRS):
            sl = slice(p * LANES, (p + 1) * LANES)
            items.append(dict(b=b, p=p, sl=sl, r=r[:, sl], k=k[:, sl], v=v[:, sl], a=a[:, sl], ld=logd[:, sl],
                              kk=kk_raw[:, sl], rkr=rkr[:, sl], g=g[:, sl]))

    def each(fn):
        return [fn(it) for it in items]

    def put(name, vals):
        for it, val in zip(items, vals):
            it[name] = val

    put("ss", each(lambda it: _dot_exact_rhs(it["kk"] * it["kk"], ones_bd)))
    put("lcum", each(lambda it: _dot_exact_lhs(tri, it["ld"])))
    put("bonus", each(lambda it: _dot_exact_rhs(it["rkr"], ones_bd)))
    for it in items:
        kk_n = it["kk"] / jnp.maximum(jnp.sqrt(it["ss"]), 1e-12)
        lcum = it["lcum"]
        l_end = lcum[CHUNK - 1:CHUNK, :]
        p_inv = jnp.exp(-lcum)
        to_end = jnp.exp(l_end - lcum)
        beta = kk_n * it["a"]
        it["l_end"] = l_end
        it["rt"] = it["r"] * jnp.exp(lcum)
        it["kt"] = kk_n * jnp.exp(lcum - it["ld"])
        it["kh_end"] = it["k"] * to_end
        it["bh_end"] = beta * to_end
        it["lhs"] = jnp.concatenate([it["kt"], it["rt"]], axis=0).astype(BF16)
        it["rhs"] = jnp.concatenate([_bd(it["k"] * p_inv), _bd(beta * p_inv)], axis=0).astype(BF16)
    put("sc", each(lambda it: _dot_nt(it["lhs"], it["rhs"])))
    for it in items:
        sc = it["sc"]
        it["a_vk"] = jnp.where(strict, sc[:CHUNK, :LANES], 0.0)
        it["n_w"] = jnp.where(strict, sc[:CHUNK, LANES:], 0.0)
        it["a_rk"] = jnp.where(incl, sc[CHUNK:, :LANES], 0.0)
        it["a_rb"] = jnp.where(incl, sc[CHUNK:, LANES:], 0.0)
        it["d1"] = jnp.where(same16, it["n_w"], 0.0)

    put("av", each(lambda it: _mm(it["a_vk"], it["v"])))
    put("d2", each(lambda it: _mm(it["d1"], it["d1"])))
    put("d4", each(lambda it: _mm(it["d2"], it["d2"])))
    put("t", each(lambda it: _mm(eye_w - it["d1"], eye_w + it["d2"])))
    put("d8", each(lambda it: _mm(it["d4"], it["d4"])))
    put("t", each(lambda it: _mm(it["t"], eye_w + it["d4"])))
    put("t", each(lambda it: _mm(it["t"], eye_w + it["d8"])))
    off32_mask = jnp.logical_and(same32, jnp.logical_not(same16))
    put("tmp", each(lambda it: _mm(it["t"], jnp.where(off32_mask, it["n_w"], 0.0))))
    put("t", each(lambda it: it["t"] - _mm(it["tmp"], it["t"])))
    put("tmp", each(lambda it: _mm(it["t"], jnp.where(same32, 0.0, it["n_w"]))))
    put("t", each(lambda it: it["t"] - _mm(it["tmp"], it["t"])))

    put("x_w", each(lambda it: _mm(it["t"], it["av"])))
    put("w_w", each(lambda it: _mm(it["t"], it["kt"])))
    put("st", each(lambda it: st_ref[it["b"], it["p"]]))
    put("st_b", each(lambda it: it["st"].astype(BF16)))
    put("u", each(lambda it: -(_dot(it["w_w"].astype(BF16), it["st_b"]) + it["x_w"])))
    put("y", each(lambda it: _dot(it["rt"].astype(BF16), it["st_b"]) + _mm(it["a_rk"], it["v"])
                  + _mm(it["a_rb"], it["u"])))
    put("d_st", each(lambda it: _dot_tn(jnp.concatenate([it["kh_end"], it["bh_end"]], axis=0).astype(BF16),
                                        jnp.concatenate([it["v"], it["u"]], axis=0).astype(BF16))))
    for it in items:
        p_col = jnp.sum(jnp.where(eye2, jnp.broadcast_to(jnp.exp(it["l_end"]), (LANES, LANES)), 0.0),
                        axis=1, keepdims=True)
        st_ref[it["b"], it["p"]] = it["st"] * p_col + jnp.where(blk, it["d_st"], 0.0)

    put("mean", each(lambda it: _dot_exact_rhs(it["y"], ones_bd) * (1.0 / RW_HEAD_DIM)))
    put("yc", each(lambda it: it["y"] - it["mean"]))
    put("var", each(lambda it: _dot_exact_rhs(it["yc"] * it["yc"], ones_bd) * (1.0 / RW_HEAD_DIM)))
    for it in items:
        yn = it["yc"] * lax.rsqrt(it["var"] + RW_GN_EPS) * gnw_ref[:, it["sl"]] + gnb_ref[:, it["sl"]]
        y_ref[it["b"], :, it["sl"]] = (yn + it["bonus"] * it["v"]) * it["g"]


def _rwkv(z_rw, mu, w0, wup, a0, aup, gup, k_k, k_a, r_k, gn_w, gn_b, nb=2):
    bsz, t_len, _ = z_rw.shape
    full = lambda a: pl.BlockSpec(a.shape, lambda b, c: (0,) * a.ndim)
    params = (mu, w0, wup, a0, aup, gup, k_k, k_a, r_k, gn_w, gn_b)
    return pl.pallas_call(
        _rwkv_kernel,
        grid=(bsz // nb, t_len // CHUNK),
        in_specs=[pl.BlockSpec((nb, CHUNK, RW_COLS), lambda b, c: (b, c, 0))] + [full(a) for a in params],
        out_specs=pl.BlockSpec((nb, CHUNK, RW_WIDTH), lambda b, c: (b, c, 0)),
        out_shape=jax.ShapeDtypeStruct((bsz, t_len, RW_WIDTH), F32),
        scratch_shapes=[pltpu.VMEM((nb, 8, RW_COLS), F32), pltpu.VMEM((nb, RW_PAIRS, LANES, LANES), F32)],
        compiler_params=pltpu.CompilerParams(dimension_semantics=("parallel", "arbitrary"),
                                             vmem_limit_bytes=VMEM_LIMIT),
        name="rwkv7",
    )(z_rw, *params)


def _mlstm_kernel(z_ref, zif_ref, cw_ref, cb_ref, bif_ref, nw_ref, h_ref, conv_ref, c_ref, n_ref, m_ref):
    c = pl.program_id(1)
    nb = z_ref.shape[0]

    @pl.when(c == 0)
    def _():
        conv_ref[:, 0:8, :] = jnp.zeros((nb, 8, 2 * ML_WIDTH), F32)
        c_ref[...] = jnp.zeros_like(c_ref)
        n_ref[...] = jnp.zeros_like(n_ref)
        m_ref[...] = jnp.zeros_like(m_ref)

    tri = (_iota((CHUNK, CHUNK), 1) <= _iota((CHUNK, CHUNK), 0))
    tri_b = tri.astype(BF16)
    items = []
    for b in range(nb):
        conv_ref[b, 8:8 + CHUNK, :] = z_ref[b, :, 0:2 * ML_WIDTH]
        acc = jnp.broadcast_to(cb_ref[...], (CHUNK, 2 * ML_WIDTH))
        for j in range(ML_CONV):
            off = 8 - (ML_CONV - 1) + j
            acc = acc + cw_ref[j:j + 1, :] * conv_ref[b, off:off + CHUNK, :]
        conv_ref[b, 0:8, :] = conv_ref[b, CHUNK:CHUNK + 8, :]
        qk = acc * _sigmoid(acc)
        q = qk[:, :ML_WIDTH]
        k = qk[:, ML_WIDTH:] * (ML_HEAD_DIM ** -0.5)
        v = z_ref[b, :, 2 * ML_WIDTH:3 * ML_WIDTH]
        o_pre = z_ref[b, :, 3 * ML_WIDTH:4 * ML_WIDTH]

        gz = zif_ref[b] + bif_ref[...]
        lf = -_softplus(-gz)
        bcum = _dot_exact_lhs(tri_b, lf)
        gz_t = gz.T
        bcum_t = bcum.T
        for h in range(ML_HEADS):
            sl = slice(h * ML_HEAD_DIM, (h + 1) * ML_HEAD_DIM)
            fh = ML_HEADS + h
            items.append(dict(b=b, h=h, sl=sl, q=q[:, sl], k=k[:, sl], v=v[:, sl], o=o_pre[:, sl],
                              bc_col=bcum[:, fh:fh + 1], bc_row=bcum_t[fh:fh + 1, :],
                              ig_col=gz[:, h:h + 1], ig_row=gz_t[h:h + 1, :],
                              g_end=bcum[CHUNK - 1:CHUNK, fh:fh + 1]))

    def each(fn):
        return [fn(it) for it in items]

    def put(name, vals):
        for it, val in zip(items, vals):
            it[name] = val

    for it in items:
        it["qb"] = it["q"].astype(BF16)
        it["kb"] = it["k"].astype(BF16)
        it["vb"] = it["v"].astype(BF16)
        it["d_log"] = jnp.where(tri, it["bc_col"] - it["bc_row"] + it["ig_row"], NEG_INF)
        a_row = it["g_end"] - it["bc_row"] + it["ig_row"]
        it["m_loc"] = jnp.max(a_row, axis=1, keepdims=True)
        wts = jnp.exp(it["g_end"] - it["bc_col"] + it["ig_col"] - it["m_loc"])
        it["k_w"] = it["k"] * wts
        it["c_in"] = c_ref[it["b"], it["h"]]
        it["n_in"] = n_ref[it["b"], it["h"], 0:1, :]
        it["m_in"] = m_ref[it["b"], it["h"], 0:1, 0:1]
    put("qk", each(lambda it: _dot_nt(it["qb"], it["kb"])))
    put("qc", each(lambda it: _dot_nt(it["qb"], it["c_in"].astype(BF16))))
    put("c_loc", each(lambda it: _dot_tn(it["vb"], it["k_w"].astype(BF16))))
    for it in items:
        inter_log = it["bc_col"] + it["m_in"]
        m_t = jnp.maximum(jnp.max(it["d_log"], axis=1, keepdims=True), inter_log)
        it["m_t"] = m_t
        it["scores"] = it["qk"] * jnp.exp(it["d_log"] - m_t)
        it["inter_w"] = jnp.exp(inter_log - m_t)
    put("sv", each(lambda it: _dot(it["scores"].astype(BF16), it["vb"])))
    for it in items:
        num = it["sv"] + it["inter_w"] * it["qc"]
        den = jnp.sum(it["scores"], axis=1, keepdims=True) \
            + it["inter_w"] * jnp.sum(it["q"] * it["n_in"], axis=1, keepdims=True)
        hh = num / jnp.maximum(jnp.abs(den), jnp.exp(-it["m_t"]))
        mean = jnp.mean(hh, axis=1, keepdims=True)
        hc = hh - mean
        var = jnp.mean(hc * hc, axis=1, keepdims=True)
        hn = hc * lax.rsqrt(var + ML_NORM_EPS) * nw_ref[:, it["sl"]]
        h_ref[it["b"], :, it["sl"]] = hn * _sigmoid(it["o"])

        m_new = jnp.maximum(it["g_end"] + it["m_in"], it["m_loc"])
        s_old = jnp.exp(it["g_end"] + it["m_in"] - m_new)
        s_new = jnp.exp(it["m_loc"] - m_new)
        n_loc = jnp.sum(it["k_w"], axis=0, keepdims=True)
        c_ref[it["b"], it["h"]] = s_old * it["c_in"] + s_new * it["c_loc"]
        n_ref[it["b"], it["h"]] = jnp.broadcast_to(s_old * it["n_in"] + s_new * n_loc, (8, ML_HEAD_DIM))
        m_ref[it["b"], it["h"]] = jnp.broadcast_to(m_new, (8, LANES))


def _mlstm(z_ml, z_if, conv_w, conv_b, b_if, norm_w, nb=2):
    bsz, t_len, _ = z_ml.shape
    full = lambda a: pl.BlockSpec(a.shape, lambda b, c: (0,) * a.ndim)
    params = (conv_w, conv_b, b_if, norm_w)
    return pl.pallas_call(
        _mlstm_kernel,
        grid=(bsz // nb, t_len // CHUNK),
        in_specs=[pl.BlockSpec((nb, CHUNK, ML_MAIN), lambda b, c: (b, c, 0)),
                  pl.BlockSpec((nb, CHUNK, LANES), lambda b, c: (b, c, 0))] + [full(a) for a in params],
        out_specs=pl.BlockSpec((nb, CHUNK, ML_WIDTH), lambda b, c: (b, c, 0)),
        out_shape=jax.ShapeDtypeStruct((bsz, t_len, ML_WIDTH), F32),
        scratch_shapes=[pltpu.VMEM((nb, 8 + CHUNK, 2 * ML_WIDTH), F32),
                        pltpu.VMEM((nb, ML_HEADS, ML_HEAD_DIM, ML_HEAD_DIM), F32),
                        pltpu.VMEM((nb, ML_HEADS, 8, ML_HEAD_DIM), F32),
                        pltpu.VMEM((nb, ML_HEADS, 8, LANES), F32)],
        compiler_params=pltpu.CompilerParams(dimension_semantics=("parallel", "arbitrary"),
                                             vmem_limit_bytes=VMEM_LIMIT),
        name="mlstm",
    )(z_ml, z_if, *params)


def _merge_kernel(x_ref, yrw_ref, yml_ref, zg_ref, gb_ref, prw_ref, pml_ref, wout_ref, n2_ref, wq_ref,
                  x1_ref, tok_ref, q_ref):
    gate = _sigmoid(zg_ref[...] + gb_ref[...])
    m_rw = _dot(yrw_ref[...].astype(BF16), prw_ref[...])
    m_ml = _dot(yml_ref[...].astype(BF16), pml_ref[...])
    merged = gate[:, :D_MODEL] * m_rw + gate[:, D_MODEL:] * m_ml
    x1 = x_ref[...] + _dot(merged.astype(BF16), wout_ref[...])
    x1_ref[...] = x1
    tok = x1 * lax.rsqrt(jnp.mean(x1 * x1, axis=-1, keepdims=True) + RMS_EPS) * n2_ref[...]
    tok_b = tok.astype(BF16)
    tok_ref[...] = tok_b
    q_ref[...] = _dot(tok_b, wq_ref[...])


def _merge(x2, y_rw, y_ml, z_gate, gate_b, p_rw, p_ml, w_out, n2, w_q, tm=256):
    n = x2.shape[0]
    row = lambda w: pl.BlockSpec((tm, w), lambda i: (i, 0))
    full = lambda a: pl.BlockSpec(a.shape, lambda i: (0,) * a.ndim)
    qd = w_q.shape[1]
    return pl.pallas_call(
        _merge_kernel,
        grid=(n // tm,),
        in_specs=[row(D_MODEL), row(RW_WIDTH), row(ML_WIDTH), row(2 * D_MODEL), full(gate_b), full(p_rw),
                  full(p_ml), full(w_out), full(n2), full(w_q)],
        out_specs=[row(D_MODEL), row(D_MODEL), row(qd)],
        out_shape=[jax.ShapeDtypeStruct((n, D_MODEL), F32), jax.ShapeDtypeStruct((n, D_MODEL), BF16),
                   jax.ShapeDtypeStruct((n, qd), F32)],
        compiler_params=pltpu.CompilerParams(dimension_semantics=("parallel",), vmem_limit_bytes=VMEM_LIMIT),
        name="merge",
    )(x2, y_rw, y_ml, z_gate, gate_b, p_rw, p_ml, w_out, n2, w_q)


_CELLS = [(r0, r1) for r0 in range(PEER_TOPK) for r1 in range(PEER_TOPK) if (r0 + 1) * (r1 + 1) <= PEER_TOPK]
_CELL_ROWS = 56


def _top16_exact(s):
    kiota = _iota(s.shape, 0)
    rank = jnp.full(s.shape, float(PEER_KEYS - 1), F32)
    tops = []
    for r in range(PEER_TOPK):
        m = jnp.max(s, axis=0, keepdims=True)
        idx = jnp.min(jnp.where(s == m, kiota, PEER_KEYS), axis=0, keepdims=True)
        hit = kiota == idx
        rank = jnp.where(hit, float(r), rank)
        s = jnp.where(hit, NEG_INF, s)
        tops.append(m)
    return rank, jnp.concatenate(tops, axis=0)


def _top16_distinct(s):
    rank = jnp.full(s.shape, float(PEER_KEYS - 1), F32)
    tops = []
    for r in range(PEER_TOPK):
        m = jnp.max(s, axis=0, keepdims=True)
        hit = s == m
        rank = jnp.where(hit, float(r), rank)
        s = jnp.where(hit, NEG_INF, s)
        tops.append(m)
    return rank, jnp.concatenate(tops, axis=0)


def _select_kernel(q_ref, sk_ref, a_ref, l_ref, b_ref, c_ref, s_ref, rank_ref, tops_ref):
    tn = q_ref.shape[0]
    groups = [(h, half) for h in range(PEER_HEADS) for half in range(2)]

    n_ranked = []
    for g, (h, half) in enumerate(groups):
        base = h * PEER_QDIM + half * PEER_HALF
        s = _dot_nt(sk_ref[h, half], q_ref[:, base:base + PEER_HALF])
        s_ref[g] = s
        rank, tops = _top16_distinct(s)
        rank_ref[g] = rank
        tops_ref[g] = tops
        n_ranked.append(jnp.sum((rank < float(PEER_TOPK)).astype(F32), axis=0, keepdims=True))
    tied = [jnp.max(nr) > float(PEER_TOPK) for nr in n_ranked]

    for g in range(len(groups)):
        @pl.when(tied[g])
        def _():
            rank_x, tops_x = _top16_exact(s_ref[g])
            rank_ref[g] = rank_x
            tops_ref[g] = tops_x

    pos_col = jnp.concatenate(
        [jnp.full((1, 1), r0 * PEER_TOPK + r1, jnp.int32) for r0, r1 in _CELLS]
        + [jnp.full((_CELL_ROWS - len(_CELLS), 1), 1 << 20, jnp.int32)], axis=0)
    pos = jnp.broadcast_to(pos_col, (_CELL_ROWS, tn))
    for h in range(PEER_HEADS):
        tops0 = tops_ref[2 * h]
        tops1 = tops_ref[2 * h + 1]
        cand = jnp.concatenate([tops0[r0:r0 + 1] + tops1[r1:r1 + 1] for r0, r1 in _CELLS]
                               + [jnp.full((_CELL_ROWS - len(_CELLS), tn), NEG_INF, F32)], axis=0)
        work = cand
        sel = jnp.zeros(cand.shape, jnp.bool_)
        for _ in range(PEER_TOPK):
            m = jnp.max(work, axis=0, keepdims=True)
            idx = jnp.min(jnp.where(work == m, pos, 1 << 20), axis=0, keepdims=True)
            hit = pos == idx
            sel = jnp.logical_or(sel, hit)
            work = jnp.where(hit, NEG_INF, work)
        cmax = tops0[0:1] + tops1[0:1]
        e_sel = jnp.where(sel, jnp.exp(cand - cmax), 0.0)
        z = jnp.sum(e_sel, axis=0, keepdims=True)
        self_f = sel.astype(F32)
        counts = []
        start = 0
        for r0 in range(PEER_TOPK):
            n_cells = PEER_TOPK // (r0 + 1)
            counts.append(jnp.sum(self_f[start:start + n_cells, :], axis=0, keepdims=True))
            start += n_cells

        rank0 = rank_ref[2 * h]
        cnt = jnp.zeros(rank0.shape, F32)
        for r0 in range(PEER_TOPK):
            cnt = jnp.where(rank0 == float(r0), counts[r0], cnt)
        a_ref[h] = jnp.exp(s_ref[2 * h] - tops0[0:1]) / z
        l_ref[h] = cnt
        b_ref[h] = jnp.exp(s_ref[2 * h + 1] - tops1[0:1]).astype(BF16)
        c_ref[h] = rank_ref[2 * h + 1].astype(BF16)


def _peer_select(q, sub_keys, tn=256):
    n = q.shape[0]
    shape = (PEER_HEADS, PEER_KEYS, n)
    ospec = pl.BlockSpec((PEER_HEADS, PEER_KEYS, tn), lambda i: (0, 0, i))
    return pl.pallas_call(
        _select_kernel,
        grid=(n // tn,),
        in_specs=[pl.BlockSpec((tn, q.shape[1]), lambda i: (i, 0)),
                  pl.BlockSpec(sub_keys.shape, lambda i: (0, 0, 0, 0))],
        out_specs=[ospec] * 4,
        out_shape=[jax.ShapeDtypeStruct(shape, F32), jax.ShapeDtypeStruct(shape, F32),
                   jax.ShapeDtypeStruct(shape, BF16), jax.ShapeDtypeStruct(shape, BF16)],
        scratch_shapes=[pltpu.VMEM((2 * PEER_HEADS, PEER_KEYS, tn), F32),
                        pltpu.VMEM((2 * PEER_HEADS, PEER_KEYS, tn), F32),
                        pltpu.VMEM((2 * PEER_HEADS, PEER_TOPK, tn), F32)],
        compiler_params=pltpu.CompilerParams(dimension_semantics=("parallel",), vmem_limit_bytes=VMEM_LIMIT),
        name="peer_select",
    )(q, sub_keys)


GATE_COLS = 256
ROW_SLAB = 8


def _gelu(x):
    return 0.5 * x * (1.0 + lax.erf(x * (1.0 / math.sqrt(2.0))))


def _gate_block(slab0, row0, il, cols, rows_ref, b_ref, c_ref, h_ref, m_ref):
    zero = jnp.zeros((PEER_KEYS, GATE_COLS), BF16)
    rows = slice(il * PEER_KEYS, (il + 1) * PEER_KEYS)
    r = row0 + il
    gate = zero
    for h in range(PEER_HEADS):
        a_b = jnp.broadcast_to(rows_ref[slab0 + h, r:r + 1, cols], zero.shape).astype(BF16)
        l_b = jnp.broadcast_to(rows_ref[slab0 + PEER_HEADS + h, r:r + 1, cols], zero.shape).astype(BF16)
        gate = gate + jnp.where(c_ref[h, :, cols] < l_b, b_ref[h, :, cols], zero) * a_b
    m_ref[rows, cols] = _gelu(h_ref[rows, cols].astype(BF16)) * gate


def _dense_kernel(tokt_ref, u_ref, vt_ref, a_ref, l_ref, b_ref, c_ref, x1_ref, fg_ref, out_ref,
                  acc_ref, h0_ref, h1_ref, m0_ref, m1_ref, rows_ref):
    e = pl.program_id(1)
    te, tn = h0_ref.shape
    n_tiles = PEER_EXPERTS // te
    rows_per_tile = te // PEER_KEYS
    assert 2 * rows_per_tile == ROW_SLAB
    slab_rows = (pl.multiple_of(ROW_SLAB * jnp.maximum(e - 1, 0), ROW_SLAB),
                 pl.multiple_of(ROW_SLAB * jnp.minimum(e, n_tiles // 2 - 1), ROW_SLAB))

    for half, i0 in enumerate(slab_rows):
        for h in range(PEER_HEADS):
            slab = half * 2 * PEER_HEADS + h
            rows_ref[slab] = a_ref[h, pl.ds(i0, ROW_SLAB), :]
            rows_ref[slab + PEER_HEADS] = l_ref[h, pl.ds(i0, ROW_SLAB), :]

    @pl.when(e == 0)
    def _():
        acc_ref[...] = jnp.zeros_like(acc_ref)
        m0_ref[...] = jnp.zeros_like(m0_ref)
        h1_ref[...] = jnp.zeros_like(h1_ref)

    halves = ((slice(0, te), m0_ref, h0_ref, h1_ref, m1_ref), (slice(te, 2 * te), m1_ref, h1_ref, h0_ref, m0_ref))
    for half, (tile, m_down, h_up, h_gate, m_gate) in enumerate(halves):
        for cg in range(tn // GATE_COLS):
            cols = slice(cg * GATE_COLS, (cg + 1) * GATE_COLS)
            for il in range(rows_per_tile):
                _gate_block(half * 2 * PEER_HEADS, (1 - half) * rows_per_tile, il, cols, rows_ref, b_ref, c_ref,
                            h_gate, m_gate)
                if il == 1:
                    acc_ref[:, cols] += _dot(vt_ref[:, tile], m_down[:, cols])
            h_up[:, cols] = _dot(u_ref[tile, :], tokt_ref[:, cols])

    @pl.when(e == pl.num_programs(1) - 1)
    def _():
        x2 = x1_ref[...] + acc_ref[...].T
        out_ref[...] = x2 * lax.rsqrt(jnp.mean(x2 * x2, axis=-1, keepdims=True) + RMS_EPS) * fg_ref[...]


def _peer_dense(tok_t, u_b, v_t, a, l, b, c, x1, final_g, tn=512, te=512):
    n = x1.shape[0]
    n_steps = PEER_EXPERTS // (2 * te) + 1
    sel_spec = pl.BlockSpec((PEER_HEADS, PEER_KEYS, tn), lambda t, e: (0, 0, t))
    return pl.pallas_call(
        _dense_kernel,
        grid=(n // tn, n_steps),
        in_specs=[pl.BlockSpec((D_MODEL, tn), lambda t, e: (0, t)),
                  pl.BlockSpec((2 * te, D_MODEL), lambda t, e: (jnp.minimum(e, n_steps - 2), 0)),
                  pl.BlockSpec((D_MODEL, 2 * te), lambda t, e: (0, jnp.maximum(e - 1, 0))),
                  sel_spec, sel_spec, sel_spec, sel_spec,
                  pl.BlockSpec((tn, D_MODEL), lambda t, e: (t, 0)),
                  pl.BlockSpec((1, D_MODEL), lambda t, e: (0, 0))],
        out_specs=pl.BlockSpec((tn, D_MODEL), lambda t, e: (t, 0)),
        out_shape=jax.ShapeDtypeStruct((n, D_MODEL), F32),
        scratch_shapes=[pltpu.VMEM((D_MODEL, tn), F32), pltpu.VMEM((te, tn), F32), pltpu.VMEM((te, tn), F32),
                        pltpu.VMEM((te, tn), BF16), pltpu.VMEM((te, tn), BF16),
                        pltpu.VMEM((4 * PEER_HEADS, ROW_SLAB, tn), F32)],
        compiler_params=pltpu.CompilerParams(dimension_semantics=("parallel", "arbitrary"),
                                             vmem_limit_bytes=VMEM_LIMIT),
        name="peer_dense",
    )(tok_t, u_b, v_t, a, l, b, c, x1, final_g)


def _row(v):
    return v.reshape(1, -1).astype(F32)


def kernel(x, norm1_g, w_in, rw_mu, rw_w0, rw_w_up, rw_a0, rw_a_up, rw_g_up, rw_k_k, rw_k_a, rw_r_k, rw_gn_w, rw_gn_b, ml_conv_q_w, ml_conv_q_b, ml_conv_k_w, ml_conv_k_b, ml_b_i, ml_b_f, ml_norm_w, gate_b, p_rw, p_ml, w_out, norm2_g, peer_w_q, peer_sub_keys, peer_u, peer_v, final_g):
    bsz, t_len, d = x.shape
    n = bsz * t_len
    x2 = x.reshape(n, d)
    l = 0

    w = w_in[l]
    ml0 = RW_COLS
    if0 = RW_COLS + ML_MAIN
    gt0 = if0 + 2 * ML_HEADS
    wrw = w[:, :ml0].astype(BF16)
    wml = w[:, ml0:if0].astype(BF16)
    wif = jnp.pad(w[:, if0:gt0], ((0, 0), (0, LANES - 2 * ML_HEADS))).astype(BF16)
    wgt = w[:, gt0:].astype(BF16)
    z_rw, z_ml, z_if, z_gate = _inproj(x2, _row(norm1_g[l]), wrw, wml, wif, wgt)

    zero_lora = jnp.zeros((RW_DECAY_LORA, RW_WIDTH), F32)
    wup = jnp.concatenate([rw_w_up[l], zero_lora], axis=0).astype(BF16)
    aup = jnp.concatenate([zero_lora, rw_a_up[l]], axis=0).astype(BF16)
    y_rw = _rwkv(z_rw.reshape(bsz, t_len, RW_COLS), _row(rw_mu[l]), _row(rw_w0[l]), wup, _row(rw_a0[l]), aup,
                 rw_g_up[l].astype(BF16), _row(rw_k_k[l]), _row(rw_k_a[l]), _row(rw_r_k[l]), _row(rw_gn_w[l]),
                 _row(rw_gn_b[l]))

    conv_w = jnp.concatenate([ml_conv_q_w[l], ml_conv_k_w[l]], axis=1)
    conv_b = jnp.concatenate([ml_conv_q_b[l], ml_conv_k_b[l]]).reshape(1, -1)
    b_if = jnp.pad(jnp.concatenate([ml_b_i[l], ml_b_f[l]]), (0, LANES - 2 * ML_HEADS)).reshape(1, -1)
    y_ml = _mlstm(z_ml.reshape(bsz, t_len, ML_MAIN), z_if.reshape(bsz, t_len, LANES), conv_w, conv_b, b_if,
                  _row(ml_norm_w[l]))

    x1, tok, q = _merge(x2, y_rw.reshape(n, RW_WIDTH), y_ml.reshape(n, ML_WIDTH), z_gate, _row(gate_b[l]),
                        p_rw[l].astype(BF16), p_ml[l].astype(BF16), w_out[l].astype(BF16), _row(norm2_g[l]),
                        peer_w_q[l].astype(BF16))

    a, cnt, b, c = _peer_select(q, peer_sub_keys[l])
    out = _peer_dense(tok.T, peer_u[l].astype(BF16), peer_v[l].T.astype(BF16), a, cnt, b, c, x1, _row(final_g))
    return out.reshape(bsz, t_len, d)
```

```python
import functools
import math

import jax
import jax.numpy as jnp
from jax import lax
from jax.experimental import pallas as pl
from jax.experimental.pallas import tpu as pltpu

F32 = jnp.float32
BF16 = jnp.bfloat16

D_MODEL = 1024
CHUNK = 64
RMS_EPS = 1e-6
RW_HEADS = 8
RW_HEAD_DIM = 64
RW_WIDTH = 512
RW_DECAY_LORA = 64
RW_AAA_LORA = 64
RW_GATE_LORA = 128
RW_GN_EPS = 64e-5
RW_COLS = 3 * RW_WIDTH + RW_DECAY_LORA + RW_AAA_LORA + RW_GATE_LORA
RW_PAIRS = RW_HEADS // 2
ML_HEADS = 4
ML_HEAD_DIM = 128
ML_WIDTH = 512
ML_CONV = 4
ML_NORM_EPS = 1e-5
ML_MAIN = 4 * ML_WIDTH
PEER_HEADS = 8
PEER_KEYS = 128
PEER_EXPERTS = PEER_KEYS * PEER_KEYS
PEER_QDIM = 256
PEER_HALF = 128
PEER_TOPK = 16

LANES = 128
VMEM_LIMIT = 56 * 1024 * 1024

NEG_INF = float("-inf")


def _dot(a, b):
    return jnp.dot(a, b, preferred_element_type=F32)


def _dot_nt(a, b):
    return lax.dot_general(a, b, (((1,), (1,)), ((), ())), preferred_element_type=F32)


def _dot_tn(a, b):
    return lax.dot_general(a, b, (((0,), (0,)), ((), ())), preferred_element_type=F32)


def _sigmoid(x):
    return 1.0 / (1.0 + jnp.exp(-x))


def _softplus(x):
    return jnp.maximum(x, 0.0) + jnp.log(1.0 + jnp.exp(-jnp.abs(x)))


def _iota(shape, dim):
    return lax.broadcasted_iota(jnp.int32, shape, dim)


def _split2(x):
    hi = x.astype(BF16)
    return hi, (x - hi.astype(F32)).astype(BF16)


def _dot_exact_rhs(a, b_bf16):
    hi, lo = _split2(a)
    return _dot(hi, b_bf16) + _dot(lo, b_bf16)


def _dot_exact_lhs(a_bf16, b):
    hi = b.astype(BF16)
    rem = b - hi.astype(F32)
    mid = rem.astype(BF16)
    lo = (rem - mid.astype(F32)).astype(BF16)
    return _dot(a_bf16, hi) + _dot(a_bf16, mid) + _dot(a_bf16, lo)


def _inproj_kernel(x_ref, g_ref, wrw_ref, wml_ref, wif_ref, wgt_ref, zrw_ref, zml_ref, zif_ref, zgt_ref):
    x = x_ref[...]
    h = x * lax.rsqrt(jnp.mean(x * x, axis=-1, keepdims=True) + RMS_EPS) * g_ref[...]
    hb = h.astype(BF16)
    zrw_ref[...] = _dot(hb, wrw_ref[...])
    zml_ref[...] = _dot(hb, wml_ref[...])
    zif_ref[...] = _dot(hb, wif_ref[...])
    zgt_ref[...] = _dot(hb, wgt_ref[...])


def _inproj(x2, g, wrw, wml, wif, wgt, tm=256):
    n = x2.shape[0]
    row = lambda w: pl.BlockSpec((tm, w), lambda i: (i, 0))
    full = lambda a: pl.BlockSpec(a.shape, lambda i: (0,) * a.ndim)
    return pl.pallas_call(
        _inproj_kernel,
        grid=(n // tm,),
        in_specs=[row(D_MODEL), full(g), full(wrw), full(wml), full(wif), full(wgt)],
        out_specs=[row(wrw.shape[1]), row(wml.shape[1]), row(wif.shape[1]), row(wgt.shape[1])],
        out_shape=[jax.ShapeDtypeStruct((n, w.shape[1]), F32) for w in (wrw, wml, wif, wgt)],
        compiler_params=pltpu.CompilerParams(dimension_semantics=("parallel",), vmem_limit_bytes=VMEM_LIMIT),
        name="inproj",
    )(x2, g, wrw, wml, wif, wgt)


def _bd(xw):
    xb = xw.astype(BF16)
    left = _iota(xb.shape, 1) < RW_HEAD_DIM
    zero = jnp.zeros_like(xb)
    return jnp.concatenate([jnp.where(left, xb, zero), jnp.where(left, zero, xb)], axis=0)


def _mm(xw, yw):
    return _dot(xw.astype(BF16), _bd(yw))


def _rwkv_kernel(z_ref, mu_ref, w0_ref, wup_ref, a0_ref, aup_ref, gup_ref, kk_ref, ka_ref, rk_ref,
                 gnw_ref, gnb_ref, y_ref, prev_ref, st_ref):
    c = pl.program_id(1)
    nb = z_ref.shape[0]

    @pl.when(c == 0)
    def _():
        prev_ref[...] = jnp.zeros_like(prev_ref)
        st_ref[...] = jnp.zeros_like(st_ref)

    ii = _iota((CHUNK, LANES), 0)
    jj = _iota((CHUNK, LANES), 1) % RW_HEAD_DIM
    strict = jj < ii
    incl = jj <= ii
    eye_w = (jj == ii).astype(F32)
    same16 = (ii // 16) == (jj // 16)
    same32 = (ii // 32) == (jj // 32)
    tri = (_iota((CHUNK, CHUNK), 1) <= _iota((CHUNK, CHUNK), 0)).astype(BF16)
    r2 = _iota((LANES, LANES), 0)
    c2 = _iota((LANES, LANES), 1)
    blk = (r2 // RW_HEAD_DIM) == (c2 // RW_HEAD_DIM)
    ones_bd = blk.astype(BF16)
    eye2 = r2 == c2
    first = _iota((CHUNK, RW_COLS), 0) == 0
    o1, o2, o3 = RW_WIDTH, 2 * RW_WIDTH, 3 * RW_WIDTH

    items = []
    for b in range(nb):
        z = z_ref[b]
        z_prev = jnp.where(first, jnp.broadcast_to(prev_ref[b, 0:1, :], z.shape), pltpu.roll(z, 1, 0))
        prev_ref[b, 0:1, :] = z[CHUNK - 1:CHUNK, :]
        zs = z + (z_prev - z) * mu_ref[...]
        r = zs[:, :o1]
        k = zs[:, o1:o2]
        v = zs[:, o2:o3]
        zwa = zs[:, o3:o3 + LANES]
        zg = zs[:, o3 + LANES:]
        w_log = -_softplus(-(w0_ref[...] + _dot(jnp.tanh(zwa).astype(BF16), wup_ref[...]))) - 0.5
        logd = -jnp.exp(w_log)
        a = _sigmoid(a0_ref[...] + _dot(zwa.astype(BF16), aup_ref[...]))
        g = _dot(_sigmoid(zg).astype(BF16), gup_ref[...])
        kk_raw = k * kk_ref[...]
        k = k * (1.0 + (a - 1.0) * ka_ref[...])
        rkr = r * k * rk_ref[...]
        for p in range(RW_PAIRS):
            sl = slice(p * LANES, (p + 1) * LANES)
            items.append(dict(b=b, p=p, sl=sl, r=r[:, sl], k=k[:, sl], v=v[:, sl], a=a[:, sl], ld=logd[:, sl],
                              kk=kk_raw[:, sl], rkr=rkr[:, sl], g=g[:, sl]))

    def each(fn):
        return [fn(it) for it in items]

    def put(name, vals):
        for it, val in zip(items, vals):
            it[name] = val

    put("ss", each(lambda it: _dot_exact_rhs(it["kk"] * it["kk"], ones_bd)))
    put("lcum", each(lambda it: _dot_exact_lhs(tri, it["ld"])))
    put("bonus", each(lambda it: _dot_exact_rhs(it["rkr"], ones_bd)))
    for it in items:
        kk_n = it["kk"] / jnp.maximum(jnp.sqrt(it["ss"]), 1e-12)
        lcum = it["lcum"]
        l_end = lcum[CHUNK - 1:CHUNK, :]
        p_inv = jnp.exp(-lcum)
        to_end = jnp.exp(l_end - lcum)
        beta = kk_n * it["a"]
        it["l_end"] = l_end
        it["rt"] = it["r"] * jnp.exp(lcum)
        it["kt"] = kk_n * jnp.exp(lcum - it["ld"])
        it["kh_end"] = it["k"] * to_end
        it["bh_end"] = beta * to_end
        it["lhs"] = jnp.concatenate([it["kt"], it["rt"]], axis=0).astype(BF16)
        it["rhs"] = jnp.concatenate([_bd(it["k"] * p_inv), _bd(beta * p_inv)], axis=0)
    put("sc", each(lambda it: _dot_nt(it["lhs"], it["rhs"])))
    for it in items:
        sc = it["sc"]
        it["a_vk"] = jnp.where(strict, sc[:CHUNK, :LANES], 0.0)
        it["n_w"] = jnp.where(strict, sc[:CHUNK, LANES:], 0.0)
        it["a_rk"] = jnp.where(incl, sc[CHUNK:, :LANES], 0.0)
        it["a_rb"] = jnp.where(incl, sc[CHUNK:, LANES:], 0.0)
        it["d1"] = jnp.where(same16, it["n_w"], 0.0)

    put("av", each(lambda it: _mm(it["a_vk"], it["v"])))
    put("d2", each(lambda it: _mm(it["d1"], it["d1"])))
    put("d4", each(lambda it: _mm(it["d2"], it["d2"])))
    put("t", each(lambda it: _mm(eye_w - it["d1"], eye_w + it["d2"])))
    put("d8", each(lambda it: _mm(it["d4"], it["d4"])))
    put("t", each(lambda it: _mm(it["t"], eye_w + it["d4"])))
    put("t", each(lambda it: _mm(it["t"], eye_w + it["d8"])))
    off32_mask = jnp.logical_and(same32, jnp.logical_not(same16))
    put("tmp", each(lambda it: _mm(it["t"], jnp.where(off32_mask, it["n_w"], 0.0))))
    put("t", each(lambda it: it["t"] - _mm(it["tmp"], it["t"])))
    put("tmp", each(lambda it: _mm(it["t"], jnp.where(same32, 0.0, it["n_w"]))))
    put("t", each(lambda it: it["t"] - _mm(it["tmp"], it["t"])))

    put("x_w", each(lambda it: _mm(it["t"], it["av"])))
    put("w_w", each(lambda it: _mm(it["t"], it["kt"])))
    put("st", each(lambda it: st_ref[it["b"], it["p"]]))
    put("st_b", each(lambda it: it["st"].astype(BF16)))
    put("u", each(lambda it: -(_dot(it["w_w"].astype(BF16), it["st_b"]) + it["x_w"])))
    put("y", each(lambda it: _dot(it["rt"].astype(BF16), it["st_b"]) + _mm(it["a_rk"], it["v"])
                  + _mm(it["a_rb"], it["u"])))
    put("d_st", each(lambda it: _dot_tn(jnp.concatenate([it["kh_end"], it["bh_end"]], axis=0).astype(BF16),
                                        jnp.concatenate([it["v"], it["u"]], axis=0).astype(BF16))))
    for it in items:
        p_col = jnp.sum(jnp.where(eye2, jnp.broadcast_to(jnp.exp(it["l_end"]), (LANES, LANES)), 0.0),
                        axis=1, keepdims=True)
        st_ref[it["b"], it["p"]] = it["st"] * p_col + jnp.where(blk, it["d_st"], 0.0)

    put("mean", each(lambda it: _dot_exact_rhs(it["y"], ones_bd) * (1.0 / RW_HEAD_DIM)))
    put("yc", each(lambda it: it["y"] - it["mean"]))
    put("var", each(lambda it: _dot_exact_rhs(it["yc"] * it["yc"], ones_bd) * (1.0 / RW_HEAD_DIM)))
    for it in items:
        yn = it["yc"] * lax.rsqrt(it["var"] + RW_GN_EPS) * gnw_ref[:, it["sl"]] + gnb_ref[:, it["sl"]]
        y_ref[it["b"], :, it["sl"]] = (yn + it["bonus"] * it["v"]) * it["g"]


def _rwkv(z_rw, mu, w0, wup, a0, aup, gup, k_k, k_a, r_k, gn_w, gn_b, nb=4):
    bsz, t_len, _ = z_rw.shape
    full = lambda a: pl.BlockSpec(a.shape, lambda b, c: (0,) * a.ndim)
    params = (mu, w0, wup, a0, aup, gup, k_k, k_a, r_k, gn_w, gn_b)
    return pl.pallas_call(
        _rwkv_kernel,
        grid=(bsz // nb, t_len // CHUNK),
        in_specs=[pl.BlockSpec((nb, CHUNK, RW_COLS), lambda b, c: (b, c, 0))] + [full(a) for a in params],
        out_specs=pl.BlockSpec((nb, CHUNK, RW_WIDTH), lambda b, c: (b, c, 0)),
        out_shape=jax.ShapeDtypeStruct((bsz, t_len, RW_WIDTH), F32),
        scratch_shapes=[pltpu.VMEM((nb, 8, RW_COLS), F32), pltpu.VMEM((nb, RW_PAIRS, LANES, LANES), F32)],
        compiler_params=pltpu.CompilerParams(dimension_semantics=("parallel", "arbitrary"),
                                             vmem_limit_bytes=VMEM_LIMIT),
        name="rwkv7",
    )(z_rw, *params)


def _cummax_rows(x):
    rows = _iota(x.shape, 0)
    d = 1
    while d < x.shape[0]:
        x = jnp.maximum(x, jnp.where(rows < d, NEG_INF, pltpu.roll(x, d, 0)))
        d *= 2
    return x


def _dot_exact_rhs3(a, b_bf16):
    hi = a.astype(BF16)
    rem = a - hi.astype(F32)
    mid = rem.astype(BF16)
    lo = (rem - mid.astype(F32)).astype(BF16)
    return _dot(hi, b_bf16) + _dot(mid, b_bf16) + _dot(lo, b_bf16)


def _mlstm_kernel(z_ref, zif_ref, cw_ref, cb_ref, bif_ref, nw_ref, h_ref, conv_ref, c_ref, n_ref, m_ref):
    c = pl.program_id(1)
    nb = z_ref.shape[0]

    @pl.when(c == 0)
    def _():
        conv_ref[:, 0:8, :] = jnp.zeros((nb, 8, 2 * ML_WIDTH), F32)
        c_ref[...] = jnp.zeros_like(c_ref)
        n_ref[...] = jnp.zeros_like(n_ref)
        m_ref[...] = jnp.zeros_like(m_ref)

    tri = (_iota((CHUNK, CHUNK), 1) <= _iota((CHUNK, CHUNK), 0))
    tri_b = tri.astype(BF16)
    ones_sq = jnp.ones((LANES, LANES), BF16)
    ones_tall = jnp.ones((CHUNK, LANES), BF16)
    src = _iota((LANES, 2 * ML_HEADS * LANES), 0)
    blk_id = _iota((LANES, 2 * ML_HEADS * LANES), 1) // LANES
    head = blk_id // 2
    is_diff = (blk_id % 2) == 0
    sel = jnp.where(src == head + ML_HEADS, jnp.where(is_diff, -1.0, 1.0),
                    jnp.where(jnp.logical_and(is_diff, src == head), 1.0, 0.0)).astype(BF16)
    gate_lane = _iota((CHUNK, LANES), 1)

    items = []
    for b in range(nb):
        conv_ref[b, 8:8 + CHUNK, :] = z_ref[b, :, 0:2 * ML_WIDTH]
        acc = jnp.broadcast_to(cb_ref[...], (CHUNK, 2 * ML_WIDTH))
        for j in range(ML_CONV):
            off = 8 - (ML_CONV - 1) + j
            acc = acc + cw_ref[j:j + 1, :] * conv_ref[b, off:off + CHUNK, :]
        conv_ref[b, 0:8, :] = conv_ref[b, CHUNK:CHUNK + 8, :]
        qk = acc * _sigmoid(acc)
        q = qk[:, :ML_WIDTH]
        k = qk[:, ML_WIDTH:] * (ML_HEAD_DIM ** -0.5)
        v = z_ref[b, :, 2 * ML_WIDTH:3 * ML_WIDTH]
        o_pre = z_ref[b, :, 3 * ML_WIDTH:4 * ML_WIDTH]

        gz = zif_ref[b] + bif_ref[...]
        lf = -_softplus(-gz)
        bcum = _dot_exact_lhs(tri_b, lf)
        gates = jnp.where(gate_lane < ML_HEADS, gz, bcum)
        col_b = _dot_exact_rhs3(gates, sel)
        gates_t = gates.T
        for h in range(ML_HEADS):
            sl = slice(h * ML_HEAD_DIM, (h + 1) * ML_HEAD_DIM)
            items.append(dict(b=b, h=h, sl=sl, q=q[:, sl], k=k[:, sl], v=v[:, sl], o=o_pre[:, sl],
                              diff_col=col_b[:, 2 * h * LANES:(2 * h + 1) * LANES],
                              bc_col=col_b[:, (2 * h + 1) * LANES:(2 * h + 2) * LANES],
                              diff_row=gates_t[h:h + 1, :] - gates_t[ML_HEADS + h:ML_HEADS + h + 1, :]))

    def each(fn):
        return [fn(it) for it in items]

    def put(name, vals):
        for it, val in zip(items, vals):
            it[name] = val

    for it in items:
        it["qb"] = it["q"].astype(BF16)
        it["kb"] = it["k"].astype(BF16)
        it["c_in"] = c_ref[it["b"], it["h"]]
        it["n_in"] = n_ref[it["b"], it["h"], 0:1, :]
        it["m_in"] = m_ref[it["b"], it["h"], 0:1, :]
        it["g_end"] = it["bc_col"][CHUNK - 1:CHUNK, :]
        it["m_loc"] = jnp.max(it["diff_col"], axis=0, keepdims=True) + it["g_end"]
        wts = jnp.exp(it["diff_col"] + (it["g_end"] - it["m_loc"]))
        it["k_w"] = it["k"] * wts
        it["m_t"] = it["bc_col"] + jnp.maximum(_cummax_rows(it["diff_col"]), it["m_in"])
        d_log = jnp.where(tri, it["bc_col"][:, :CHUNK] + it["diff_row"], NEG_INF)
        it["p"] = jnp.exp(d_log - it["m_t"][:, :CHUNK])
        it["inter_w"] = jnp.exp(it["bc_col"] + it["m_in"] - it["m_t"])
        it["v_ext"] = jnp.concatenate([it["v"].astype(BF16), ones_tall], axis=1)
        n_rep = jnp.broadcast_to(it["n_in"], (ML_HEAD_DIM, ML_HEAD_DIM))
        it["c_ext"] = jnp.concatenate([it["c_in"], n_rep], axis=0).astype(BF16)
    put("qk", each(lambda it: _dot_nt(it["qb"], it["kb"])))
    put("qc", each(lambda it: _dot_nt(it["qb"], it["c_ext"])))
    put("c_loc", each(lambda it: _dot_tn(it["v_ext"][:, :ML_HEAD_DIM], it["k_w"].astype(BF16))))
    put("sv", each(lambda it: _dot((it["qk"] * it["p"]).astype(BF16), it["v_ext"])))
    for it in items:
        d = ML_HEAD_DIM
        num = it["sv"][:, :d] + it["inter_w"] * it["qc"][:, :d]
        den = it["sv"][:, d:] + it["inter_w"] * it["qc"][:, d:]
        it["hh"] = num / jnp.maximum(jnp.abs(den), jnp.exp(-it["m_t"]))
    put("mean", each(lambda it: _dot_exact_rhs(it["hh"], ones_sq) * (1.0 / ML_HEAD_DIM)))
    put("hc", each(lambda it: it["hh"] - it["mean"]))
    put("var", each(lambda it: _dot_exact_rhs(it["hc"] * it["hc"], ones_sq) * (1.0 / ML_HEAD_DIM)))
    for it in items:
        hn = it["hc"] * lax.rsqrt(it["var"] + ML_NORM_EPS) * nw_ref[:, it["sl"]]
        h_ref[it["b"], :, it["sl"]] = hn * _sigmoid(it["o"])

        m_new = jnp.maximum(it["g_end"] + it["m_in"], it["m_loc"])
        s_old = jnp.exp(it["g_end"] + it["m_in"] - m_new)
        s_new = jnp.exp(it["m_loc"] - m_new)
        n_loc = jnp.sum(it["k_w"], axis=0, keepdims=True)
        c_ref[it["b"], it["h"]] = s_old * it["c_in"] + s_new * it["c_loc"]
        n_ref[it["b"], it["h"]] = jnp.broadcast_to(s_old * it["n_in"] + s_new * n_loc, (8, ML_HEAD_DIM))
        m_ref[it["b"], it["h"]] = jnp.broadcast_to(m_new, (8, LANES))


def _mlstm(z_ml, z_if, conv_w, conv_b, b_if, norm_w, nb=4):
    bsz, t_len, _ = z_ml.shape
    full = lambda a: pl.BlockSpec(a.shape, lambda b, c: (0,) * a.ndim)
    params = (conv_w, conv_b, b_if, norm_w)
    return pl.pallas_call(
        _mlstm_kernel,
        grid=(bsz // nb, t_len // CHUNK),
        in_specs=[pl.BlockSpec((nb, CHUNK, ML_MAIN), lambda b, c: (b, c, 0)),
                  pl.BlockSpec((nb, CHUNK, LANES), lambda b, c: (b, c, 0))] + [full(a) for a in params],
        out_specs=pl.BlockSpec((nb, CHUNK, ML_WIDTH), lambda b, c: (b, c, 0)),
        out_shape=jax.ShapeDtypeStruct((bsz, t_len, ML_WIDTH), F32),
        scratch_shapes=[pltpu.VMEM((nb, 8 + CHUNK, 2 * ML_WIDTH), F32),
                        pltpu.VMEM((nb, ML_HEADS, ML_HEAD_DIM, ML_HEAD_DIM), F32),
                        pltpu.VMEM((nb, ML_HEADS, 8, ML_HEAD_DIM), F32),
                        pltpu.VMEM((nb, ML_HEADS, 8, LANES), F32)],
        compiler_params=pltpu.CompilerParams(dimension_semantics=("parallel", "arbitrary"),
                                             vmem_limit_bytes=VMEM_LIMIT),
        name="mlstm",
    )(z_ml, z_if, *params)


def _merge_kernel(x_ref, yrw_ref, yml_ref, zg_ref, gb_ref, prw_ref, pml_ref, wout_ref, n2_ref, wq_ref,
                  x1_ref, tok_ref, q_ref):
    gate = _sigmoid(zg_ref[...] + gb_ref[...])
    m_rw = _dot(yrw_ref[...].astype(BF16), prw_ref[...])
    m_ml = _dot(yml_ref[...].astype(BF16), pml_ref[...])
    merged = gate[:, :D_MODEL] * m_rw + gate[:, D_MODEL:] * m_ml
    x1 = x_ref[...] + _dot(merged.astype(BF16), wout_ref[...])
    x1_ref[...] = x1
    tok = x1 * lax.rsqrt(jnp.mean(x1 * x1, axis=-1, keepdims=True) + RMS_EPS) * n2_ref[...]
    tok_b = tok.astype(BF16)
    tok_ref[...] = tok_b
    q_ref[...] = _dot(tok_b, wq_ref[...])


def _merge(x2, y_rw, y_ml, z_gate, gate_b, p_rw, p_ml, w_out, n2, w_q, tm=256):
    n = x2.shape[0]
    row = lambda w: pl.BlockSpec((tm, w), lambda i: (i, 0))
    full = lambda a: pl.BlockSpec(a.shape, lambda i: (0,) * a.ndim)
    qd = w_q.shape[1]
    return pl.pallas_call(
        _merge_kernel,
        grid=(n // tm,),
        in_specs=[row(D_MODEL), row(RW_WIDTH), row(ML_WIDTH), row(2 * D_MODEL), full(gate_b), full(p_rw),
                  full(p_ml), full(w_out), full(n2), full(w_q)],
        out_specs=[row(D_MODEL), row(D_MODEL), row(qd)],
        out_shape=[jax.ShapeDtypeStruct((n, D_MODEL), F32), jax.ShapeDtypeStruct((n, D_MODEL), BF16),
                   jax.ShapeDtypeStruct((n, qd), F32)],
        compiler_params=pltpu.CompilerParams(dimension_semantics=("parallel",), vmem_limit_bytes=VMEM_LIMIT),
        name="merge",
    )(x2, y_rw, y_ml, z_gate, gate_b, p_rw, p_ml, w_out, n2, w_q)


_CELLS = [(r0, r1) for r0 in range(PEER_TOPK) for r1 in range(PEER_TOPK) if (r0 + 1) * (r1 + 1) <= PEER_TOPK]
_CELL_ROWS = 56


def _top16_exact(s):
    kiota = _iota(s.shape, 0)
    rank = jnp.full(s.shape, float(PEER_KEYS - 1), F32)
    tops = []
    for r in range(PEER_TOPK):
        m = jnp.max(s, axis=0, keepdims=True)
        idx = jnp.min(jnp.where(s == m, kiota, PEER_KEYS), axis=0, keepdims=True)
        hit = kiota == idx
        rank = jnp.where(hit, float(r), rank)
        s = jnp.where(hit, NEG_INF, s)
        tops.append(m)
    return rank, jnp.concatenate(tops, axis=0)


def _top16_distinct(s):
    rank = jnp.full(s.shape, float(PEER_KEYS - 1), F32)
    tops = []
    for r in range(PEER_TOPK):
        m = jnp.max(s, axis=0, keepdims=True)
        hit = s == m
        rank = jnp.where(hit, float(r), rank)
        s = jnp.where(hit, NEG_INF, s)
        tops.append(m)
    return rank, jnp.concatenate(tops, axis=0)


def _select_kernel(q_ref, sk_ref, a_ref, l_ref, b_ref, c_ref, s_ref, rank_ref, tops_ref):
    tn = q_ref.shape[0]
    groups = [(h, half) for h in range(PEER_HEADS) for half in range(2)]

    n_ranked = []
    for g, (h, half) in enumerate(groups):
        base = h * PEER_QDIM + half * PEER_HALF
        s = _dot_nt(sk_ref[h, half], q_ref[:, base:base + PEER_HALF])
        s_ref[g] = s
        rank, tops = _top16_distinct(s)
        rank_ref[g] = rank
        tops_ref[g] = tops
        n_ranked.append(jnp.sum((rank < float(PEER_TOPK)).astype(F32), axis=0, keepdims=True))
    most_ranked = functools.reduce(jnp.maximum, n_ranked)

    @pl.when(jnp.max(most_ranked) > float(PEER_TOPK))
    def _():
        for g in range(len(groups)):
            rank_g = rank_ref[g]
            n_g = jnp.sum((rank_g < float(PEER_TOPK)).astype(F32), axis=0, keepdims=True)

            @pl.when(jnp.max(n_g) > float(PEER_TOPK))
            def _():
                rank_x, tops_x = _top16_exact(s_ref[g])
                rank_ref[g] = rank_x
                tops_ref[g] = tops_x

    pos_col = jnp.concatenate(
        [jnp.full((1, 1), r0 * PEER_TOPK + r1, jnp.int32) for r0, r1 in _CELLS]
        + [jnp.full((_CELL_ROWS - len(_CELLS), 1), 1 << 20, jnp.int32)], axis=0)
    pos = jnp.broadcast_to(pos_col, (_CELL_ROWS, tn))
    for h in range(PEER_HEADS):
        tops0 = tops_ref[2 * h]
        tops1 = tops_ref[2 * h + 1]
        cand = jnp.concatenate([tops0[r0:r0 + 1] + tops1[r1:r1 + 1] for r0, r1 in _CELLS]
                               + [jnp.full((_CELL_ROWS - len(_CELLS), tn), NEG_INF, F32)], axis=0)
        work = cand
        sel = jnp.zeros(cand.shape, jnp.bool_)
        for _ in range(PEER_TOPK):
            m = jnp.max(work, axis=0, keepdims=True)
            idx = jnp.min(jnp.where(work == m, pos, 1 << 20), axis=0, keepdims=True)
            hit = pos == idx
            sel = jnp.logical_or(sel, hit)
            work = jnp.where(hit, NEG_INF, work)
        cmax = tops0[0:1] + tops1[0:1]
        e_sel = jnp.where(sel, jnp.exp(cand - cmax), 0.0)
        z = jnp.sum(e_sel, axis=0, keepdims=True)
        self_f = sel.astype(F32)
        counts = []
        start = 0
        for r0 in range(PEER_TOPK):
            n_cells = PEER_TOPK // (r0 + 1)
            counts.append(jnp.sum(self_f[start:start + n_cells, :], axis=0, keepdims=True))
            start += n_cells

        rank0 = rank_ref[2 * h]
        cnt = jnp.zeros(rank0.shape, F32)
        for r0 in range(PEER_TOPK):
            cnt = jnp.where(rank0 == float(r0), counts[r0], cnt)
        a_ref[h] = jnp.exp(s_ref[2 * h] - tops0[0:1]) / z
        l_ref[h] = cnt
        b_ref[h] = jnp.exp(s_ref[2 * h + 1] - tops1[0:1]).astype(BF16)
        c_ref[h] = rank_ref[2 * h + 1].astype(BF16)


def _peer_select(q, sub_keys, tn=256):
    n = q.shape[0]
    shape = (PEER_HEADS, PEER_KEYS, n)
    ospec = pl.BlockSpec((PEER_HEADS, PEER_KEYS, tn), lambda i: (0, 0, i))
    return pl.pallas_call(
        _select_kernel,
        grid=(n // tn,),
        in_specs=[pl.BlockSpec((tn, q.shape[1]), lambda i: (i, 0)),
                  pl.BlockSpec(sub_keys.shape, lambda i: (0, 0, 0, 0))],
        out_specs=[ospec] * 4,
        out_shape=[jax.ShapeDtypeStruct(shape, F32), jax.ShapeDtypeStruct(shape, F32),
                   jax.ShapeDtypeStruct(shape, BF16), jax.ShapeDtypeStruct(shape, BF16)],
        scratch_shapes=[pltpu.VMEM((2 * PEER_HEADS, PEER_KEYS, tn), F32),
                        pltpu.VMEM((2 * PEER_HEADS, PEER_KEYS, tn), F32),
                        pltpu.VMEM((2 * PEER_HEADS, PEER_TOPK, tn), F32)],
        compiler_params=pltpu.CompilerParams(dimension_semantics=("parallel",), vmem_limit_bytes=VMEM_LIMIT),
        name="peer_select",
    )(q, sub_keys)


GATE_COLS = 256
ROW_SLAB = 8


def _gelu(x):
    return 0.5 * x * (1.0 + lax.erf(x * (1.0 / math.sqrt(2.0))))


def _gate_block(slab0, row0, il, cols, rows_ref, b_ref, c_ref, h_ref, m_ref):
    zero = jnp.zeros((PEER_KEYS, GATE_COLS), BF16)
    rows = slice(il * PEER_KEYS, (il + 1) * PEER_KEYS)
    r = row0 + il
    gate = zero
    for h in range(PEER_HEADS):
        a_b = jnp.broadcast_to(rows_ref[slab0 + h, r:r + 1, cols], zero.shape).astype(BF16)
        l_b = jnp.broadcast_to(rows_ref[slab0 + PEER_HEADS + h, r:r + 1, cols], zero.shape).astype(BF16)
        gate = gate + jnp.where(c_ref[h, :, cols] < l_b, b_ref[h, :, cols], zero) * a_b
    m_ref[rows, cols] = _gelu(h_ref[rows, cols].astype(BF16)) * gate


def _dense_kernel(tokt_ref, u_ref, vt_ref, a_ref, l_ref, b_ref, c_ref, x1_ref, fg_ref, out_ref,
                  acc_ref, h0_ref, h1_ref, m0_ref, m1_ref, rows_ref):
    e = pl.program_id(1)
    te, tn = h0_ref.shape
    n_tiles = PEER_EXPERTS // te
    rows_per_tile = te // PEER_KEYS
    assert 2 * rows_per_tile == ROW_SLAB
    slab_rows = (pl.multiple_of(ROW_SLAB * jnp.maximum(e - 1, 0), ROW_SLAB),
                 pl.multiple_of(ROW_SLAB * jnp.minimum(e, n_tiles // 2 - 1), ROW_SLAB))

    for half, i0 in enumerate(slab_rows):
        for h in range(PEER_HEADS):
            slab = half * 2 * PEER_HEADS + h
            rows_ref[slab] = a_ref[h, pl.ds(i0, ROW_SLAB), :]
            rows_ref[slab + PEER_HEADS] = l_ref[h, pl.ds(i0, ROW_SLAB), :]

    @pl.when(e == 0)
    def _():
        acc_ref[...] = jnp.zeros_like(acc_ref)
        m0_ref[...] = jnp.zeros_like(m0_ref)
        h1_ref[...] = jnp.zeros_like(h1_ref)

    halves = ((slice(0, te), m0_ref, h0_ref, h1_ref, m1_ref), (slice(te, 2 * te), m1_ref, h1_ref, h0_ref, m0_ref))
    for half, (tile, m_down, h_up, h_gate, m_gate) in enumerate(halves):
        for cg in range(tn // GATE_COLS):
            cols = slice(cg * GATE_COLS, (cg + 1) * GATE_COLS)
            for il in range(rows_per_tile):
                _gate_block(half * 2 * PEER_HEADS, (1 - half) * rows_per_tile, il, cols, rows_ref, b_ref, c_ref,
                            h_gate, m_gate)
                if il == 1:
                    acc_ref[:, cols] += _dot(vt_ref[:, tile], m_down[:, cols])
            h_up[:, cols] = _dot(u_ref[tile, :], tokt_ref[:, cols])

    @pl.when(e == pl.num_programs(1) - 1)
    def _():
        x2 = x1_ref[...] + acc_ref[...].T
        out_ref[...] = x2 * lax.rsqrt(jnp.mean(x2 * x2, axis=-1, keepdims=True) + RMS_EPS) * fg_ref[...]


def _peer_dense(tok_t, u_b, v_t, a, l, b, c, x1, final_g, tn=512, te=512):
    n = x1.shape[0]
    n_steps = PEER_EXPERTS // (2 * te) + 1
    sel_spec = pl.BlockSpec((PEER_HEADS, PEER_KEYS, tn), lambda t, e: (0, 0, t))
    return pl.pallas_call(
        _dense_kernel,
        grid=(n // tn, n_steps),
        in_specs=[pl.BlockSpec((D_MODEL, tn), lambda t, e: (0, t)),
                  pl.BlockSpec((2 * te, D_MODEL), lambda t, e: (jnp.minimum(e, n_steps - 2), 0)),
                  pl.BlockSpec((D_MODEL, 2 * te), lambda t, e: (0, jnp.maximum(e - 1, 0))),
                  sel_spec, sel_spec, sel_spec, sel_spec,
                  pl.BlockSpec((tn, D_MODEL), lambda t, e: (t, 0)),
                  pl.BlockSpec((1, D_MODEL), lambda t, e: (0, 0))],
        out_specs=pl.BlockSpec((tn, D_MODEL), lambda t, e: (t, 0)),
        out_shape=jax.ShapeDtypeStruct((n, D_MODEL), F32),
        scratch_shapes=[pltpu.VMEM((D_MODEL, tn), F32), pltpu.VMEM((te, tn), F32), pltpu.VMEM((te, tn), F32),
                        pltpu.VMEM((te, tn), BF16), pltpu.VMEM((te, tn), BF16),
                        pltpu.VMEM((4 * PEER_HEADS, ROW_SLAB, tn), F32)],
        compiler_params=pltpu.CompilerParams(dimension_semantics=("parallel", "arbitrary"),
                                             vmem_limit_bytes=VMEM_LIMIT),
        name="peer_dense",
    )(tok_t, u_b, v_t, a, l, b, c, x1, final_g)


def _row(v):
    return v.reshape(1, -1).astype(F32)


def kernel(x, norm1_g, w_in, rw_mu, rw_w0, rw_w_up, rw_a0, rw_a_up, rw_g_up, rw_k_k, rw_k_a, rw_r_k, rw_gn_w, rw_gn_b, ml_conv_q_w, ml_conv_q_b, ml_conv_k_w, ml_conv_k_b, ml_b_i, ml_b_f, ml_norm_w, gate_b, p_rw, p_ml, w_out, norm2_g, peer_w_q, peer_sub_keys, peer_u, peer_v, final_g):
    bsz, t_len, d = x.shape
    n = bsz * t_len
    x2 = x.reshape(n, d)
    l = 0

    w = w_in[l]
    ml0 = RW_COLS
    if0 = RW_COLS + ML_MAIN
    gt0 = if0 + 2 * ML_HEADS
    wrw = w[:, :ml0].astype(BF16)
    wml = w[:, ml0:if0].astype(BF16)
    wif = jnp.pad(w[:, if0:gt0], ((0, 0), (0, LANES - 2 * ML_HEADS))).astype(BF16)
    wgt = w[:, gt0:].astype(BF16)
    z_rw, z_ml, z_if, z_gate = _inproj(x2, _row(norm1_g[l]), wrw, wml, wif, wgt)

    zero_lora = jnp.zeros((RW_DECAY_LORA, RW_WIDTH), F32)
    wup = jnp.concatenate([rw_w_up[l], zero_lora], axis=0).astype(BF16)
    aup = jnp.concatenate([zero_lora, rw_a_up[l]], axis=0).astype(BF16)
    y_rw = _rwkv(z_rw.reshape(bsz, t_len, RW_COLS), _row(rw_mu[l]), _row(rw_w0[l]), wup, _row(rw_a0[l]), aup,
                 rw_g_up[l].astype(BF16), _row(rw_k_k[l]), _row(rw_k_a[l]), _row(rw_r_k[l]), _row(rw_gn_w[l]),
                 _row(rw_gn_b[l]))

    conv_w = jnp.concatenate([ml_conv_q_w[l], ml_conv_k_w[l]], axis=1)
    conv_b = jnp.concatenate([ml_conv_q_b[l], ml_conv_k_b[l]]).reshape(1, -1)
    b_if = jnp.pad(jnp.concatenate([ml_b_i[l], ml_b_f[l]]), (0, LANES - 2 * ML_HEADS)).reshape(1, -1)
    y_ml = _mlstm(z_ml.reshape(bsz, t_len, ML_MAIN), z_if.reshape(bsz, t_len, LANES), conv_w, conv_b, b_if,
                  _row(ml_norm_w[l]))

    x1, tok, q = _merge(x2, y_rw.reshape(n, RW_WIDTH), y_ml.reshape(n, ML_WIDTH), z_gate, _row(gate_b[l]),
                        p_rw[l].astype(BF16), p_ml[l].astype(BF16), w_out[l].astype(BF16), _row(norm2_g[l]),
                        peer_w_q[l].astype(BF16))

    a, cnt, b, c = _peer_select(q, peer_sub_keys[l])
    out = _peer_dense(tok.T, peer_u[l].astype(BF16), peer_v[l].T.astype(BF16), a, cnt, b, c, x1, _row(final_g))
    return out.reshape(bsz, t_len, d)
```

```python
import functools
import math

import jax
import jax.numpy as jnp
from jax import lax
from jax.experimental import pallas as pl
from jax.experimental.pallas import tpu as pltpu

F32 = jnp.float32
BF16 = jnp.bfloat16

D_MODEL = 1024
CHUNK = 64
RMS_EPS = 1e-6
RW_HEADS = 8
RW_HEAD_DIM = 64
RW_WIDTH = 512
RW_DECAY_LORA = 64
RW_AAA_LORA = 64
RW_GATE_LORA = 128
RW_GN_EPS = 64e-5
RW_COLS = 3 * RW_WIDTH + RW_DECAY_LORA + RW_AAA_LORA + RW_GATE_LORA
RW_PAIRS = RW_HEADS // 2
ML_HEADS = 4
ML_HEAD_DIM = 128
ML_WIDTH = 512
ML_CONV = 4
ML_NORM_EPS = 1e-5
ML_MAIN = 4 * ML_WIDTH
PEER_HEADS = 8
PEER_KEYS = 128
PEER_EXPERTS = PEER_KEYS * PEER_KEYS
PEER_QDIM = 256
PEER_HALF = 128
PEER_TOPK = 16

LANES = 128
VMEM_LIMIT = 56 * 1024 * 1024

NEG_INF = float("-inf")


def _dot(a, b):
    return jnp.dot(a, b, preferred_element_type=F32)


def _dot_nt(a, b):
    return lax.dot_general(a, b, (((1,), (1,)), ((), ())), preferred_element_type=F32)


def _dot_tn(a, b):
    return lax.dot_general(a, b, (((0,), (0,)), ((), ())), preferred_element_type=F32)


def _sigmoid(x):
    return 1.0 / (1.0 + jnp.exp(-x))


def _softplus(x):
    return jnp.maximum(x, 0.0) + jnp.log(1.0 + jnp.exp(-jnp.abs(x)))


def _iota(shape, dim):
    return lax.broadcasted_iota(jnp.int32, shape, dim)


def _split2(x):
    hi = x.astype(BF16)
    return hi, (x - hi.astype(F32)).astype(BF16)


def _dot_exact_rhs(a, b_bf16):
    hi, lo = _split2(a)
    return _dot(hi, b_bf16) + _dot(lo, b_bf16)


def _dot_exact_lhs(a_bf16, b):
    hi = b.astype(BF16)
    rem = b - hi.astype(F32)
    mid = rem.astype(BF16)
    lo = (rem - mid.astype(F32)).astype(BF16)
    return _dot(a_bf16, hi) + _dot(a_bf16, mid) + _dot(a_bf16, lo)


def _inproj_kernel(x_ref, g_ref, wrw_ref, wml_ref, wif_ref, wgt_ref, zrw_ref, zml_ref, zif_ref, zgt_ref):
    x = x_ref[...]
    h = x * lax.rsqrt(jnp.mean(x * x, axis=-1, keepdims=True) + RMS_EPS) * g_ref[...]
    hb = h.astype(BF16)
    zrw_ref[...] = _dot(hb, wrw_ref[...])
    zml_ref[...] = _dot(hb, wml_ref[...])
    zif_ref[...] = _dot(hb, wif_ref[...])
    zgt_ref[...] = _dot(hb, wgt_ref[...])


def _inproj(x2, g, wrw, wml, wif, wgt, tm=512):
    n = x2.shape[0]
    row = lambda w: pl.BlockSpec((tm, w), lambda i: (i, 0))
    full = lambda a: pl.BlockSpec(a.shape, lambda i: (0,) * a.ndim)
    return pl.pallas_call(
        _inproj_kernel,
        grid=(n // tm,),
        in_specs=[row(D_MODEL), full(g), full(wrw), full(wml), full(wif), full(wgt)],
        out_specs=[row(wrw.shape[1]), row(wml.shape[1]), row(wif.shape[1]), row(wgt.shape[1])],
        out_shape=[jax.ShapeDtypeStruct((n, w.shape[1]), F32) for w in (wrw, wml, wif, wgt)],
        compiler_params=pltpu.CompilerParams(dimension_semantics=("parallel",), vmem_limit_bytes=VMEM_LIMIT),
        name="inproj",
    )(x2, g, wrw, wml, wif, wgt)


def _bd(xw):
    xb = xw.astype(BF16)
    left = _iota(xb.shape, 1) < RW_HEAD_DIM
    zero = jnp.zeros_like(xb)
    return jnp.concatenate([jnp.where(left, xb, zero), jnp.where(left, zero, xb)], axis=0)


def _mm(xw, yw):
    return _dot(xw.astype(BF16), _bd(yw))


def _rwkv_kernel(z_ref, mu_ref, w0_ref, wup_ref, a0_ref, aup_ref, gup_ref, kk_ref, ka_ref, rk_ref,
                 gnw_ref, gnb_ref, y_ref, prev_ref, st_ref):
    c = pl.program_id(1)
    nb = z_ref.shape[0]

    @pl.when(c == 0)
    def _():
        prev_ref[...] = jnp.zeros_like(prev_ref)
        st_ref[...] = jnp.zeros_like(st_ref)

    ii = _iota((CHUNK, LANES), 0)
    jj = _iota((CHUNK, LANES), 1) % RW_HEAD_DIM
    strict = jj < ii
    incl = jj <= ii
    eye_w = (jj == ii).astype(F32)
    same16 = (ii // 16) == (jj // 16)
    same32 = (ii // 32) == (jj // 32)
    tri = (_iota((CHUNK, CHUNK), 1) <= _iota((CHUNK, CHUNK), 0)).astype(BF16)
    r2 = _iota((LANES, LANES), 0)
    c2 = _iota((LANES, LANES), 1)
    blk = (r2 // RW_HEAD_DIM) == (c2 // RW_HEAD_DIM)
    ones_bd = blk.astype(BF16)
    eye2 = r2 == c2
    first = _iota((CHUNK, RW_COLS), 0) == 0
    o1, o2, o3 = RW_WIDTH, 2 * RW_WIDTH, 3 * RW_WIDTH

    items = []
    for b in range(nb):
        z = z_ref[b]
        z_prev = jnp.where(first, jnp.broadcast_to(prev_ref[b, 0:1, :], z.shape), pltpu.roll(z, 1, 0))
        prev_ref[b, 0:1, :] = z[CHUNK - 1:CHUNK, :]
        zs = z + (z_prev - z) * mu_ref[...]
        r = zs[:, :o1]
        k = zs[:, o1:o2]
        v = zs[:, o2:o3]
        zwa = zs[:, o3:o3 + LANES]
        zg = zs[:, o3 + LANES:]
        w_log = -_softplus(-(w0_ref[...] + _dot(jnp.tanh(zwa).astype(BF16), wup_ref[...]))) - 0.5
        logd = -jnp.exp(w_log)
        a = _sigmoid(a0_ref[...] + _dot(zwa.astype(BF16), aup_ref[...]))
        g = _dot(_sigmoid(zg).astype(BF16), gup_ref[...])
        kk_raw = k * kk_ref[...]
        k = k * (1.0 + (a - 1.0) * ka_ref[...])
        rkr = r * k * rk_ref[...]
        for p in range(RW_PAIRS):
            sl = slice(p * LANES, (p + 1) * LANES)
            items.append(dict(b=b, p=p, sl=sl, r=r[:, sl], k=k[:, sl], v=v[:, sl], a=a[:, sl], ld=logd[:, sl],
                              kk=kk_raw[:, sl], rkr=rkr[:, sl], g=g[:, sl]))

    def each(fn):
        return [fn(it) for it in items]

    def put(name, vals):
        for it, val in zip(items, vals):
            it[name] = val

    put("ss", each(lambda it: _dot_exact_rhs(it["kk"] * it["kk"], ones_bd)))
    put("lcum", each(lambda it: _dot_exact_lhs(tri, it["ld"])))
    put("bonus", each(lambda it: _dot_exact_rhs(it["rkr"], ones_bd)))
    for it in items:
        kk_n = it["kk"] / jnp.maximum(jnp.sqrt(it["ss"]), 1e-12)
        lcum = it["lcum"]
        l_end = lcum[CHUNK - 1:CHUNK, :]
        p_inv = jnp.exp(-lcum)
        to_end = jnp.exp(l_end - lcum)
        beta = kk_n * it["a"]
        it["l_end"] = l_end
        it["rt"] = it["r"] * jnp.exp(lcum)
        it["kt"] = kk_n * jnp.exp(lcum - it["ld"])
        it["kh_end"] = it["k"] * to_end
        it["bh_end"] = beta * to_end
        it["lhs"] = jnp.concatenate([it["kt"], it["rt"]], axis=0).astype(BF16)
        it["rhs"] = jnp.concatenate([_bd(it["k"] * p_inv), _bd(beta * p_inv)], axis=0)
    put("sc", each(lambda it: _dot_nt(it["lhs"], it["rhs"])))
    for it in items:
        sc = it["sc"]
        it["a_vk"] = jnp.where(strict, sc[:CHUNK, :LANES], 0.0)
        it["n_w"] = jnp.where(strict, sc[:CHUNK, LANES:], 0.0)
        it["a_rk"] = jnp.where(incl, sc[CHUNK:, :LANES], 0.0)
        it["a_rb"] = jnp.where(incl, sc[CHUNK:, LANES:], 0.0)
        it["d1"] = jnp.where(same16, it["n_w"], 0.0)

    put("av", each(lambda it: _mm(it["a_vk"], it["v"])))
    put("d2", each(lambda it: _mm(it["d1"], it["d1"])))
    put("d4", each(lambda it: _mm(it["d2"], it["d2"])))
    put("t", each(lambda it: _mm(eye_w - it["d1"], eye_w + it["d2"])))
    put("d8", each(lambda it: _mm(it["d4"], it["d4"])))
    put("t", each(lambda it: _mm(it["t"], eye_w + it["d4"])))
    put("t", each(lambda it: _mm(it["t"], eye_w + it["d8"])))
    off32_mask = jnp.logical_and(same32, jnp.logical_not(same16))
    put("tmp", each(lambda it: _mm(it["t"], jnp.where(off32_mask, it["n_w"], 0.0))))
    put("t", each(lambda it: it["t"] - _mm(it["tmp"], it["t"])))
    put("tmp", each(lambda it: _mm(it["t"], jnp.where(same32, 0.0, it["n_w"]))))
    put("t", each(lambda it: it["t"] - _mm(it["tmp"], it["t"])))

    put("x_w", each(lambda it: _mm(it["t"], it["av"])))
    put("w_w", each(lambda it: _mm(it["t"], it["kt"])))
    put("st", each(lambda it: st_ref[it["b"], it["p"]]))
    put("st_b", each(lambda it: it["st"].astype(BF16)))
    put("u", each(lambda it: -(_dot(it["w_w"].astype(BF16), it["st_b"]) + it["x_w"])))
    put("y", each(lambda it: _dot(it["rt"].astype(BF16), it["st_b"]) + _mm(it["a_rk"], it["v"])
                  + _mm(it["a_rb"], it["u"])))
    put("d_st", each(lambda it: _dot_tn(jnp.concatenate([it["kh_end"], it["bh_end"]], axis=0).astype(BF16),
                                        jnp.concatenate([it["v"], it["u"]], axis=0).astype(BF16))))
    for it in items:
        p_col = jnp.sum(jnp.where(eye2, jnp.broadcast_to(jnp.exp(it["l_end"]), (LANES, LANES)), 0.0),
                        axis=1, keepdims=True)
        st_ref[it["b"], it["p"]] = it["st"] * p_col + jnp.where(blk, it["d_st"], 0.0)

    put("mean", each(lambda it: _dot_exact_rhs(it["y"], ones_bd) * (1.0 / RW_HEAD_DIM)))
    put("yc", each(lambda it: it["y"] - it["mean"]))
    put("var", each(lambda it: _dot_exact_rhs(it["yc"] * it["yc"], ones_bd) * (1.0 / RW_HEAD_DIM)))
    for it in items:
        yn = it["yc"] * lax.rsqrt(it["var"] + RW_GN_EPS) * gnw_ref[:, it["sl"]] + gnb_ref[:, it["sl"]]
        y_ref[it["b"], :, it["sl"]] = (yn + it["bonus"] * it["v"]) * it["g"]


def _rwkv(z_rw, mu, w0, wup, a0, aup, gup, k_k, k_a, r_k, gn_w, gn_b, nb=4):
    bsz, t_len, _ = z_rw.shape
    full = lambda a: pl.BlockSpec(a.shape, lambda b, c: (0,) * a.ndim)
    params = (mu, w0, wup, a0, aup, gup, k_k, k_a, r_k, gn_w, gn_b)
    return pl.pallas_call(
        _rwkv_kernel,
        grid=(bsz // nb, t_len // CHUNK),
        in_specs=[pl.BlockSpec((nb, CHUNK, RW_COLS), lambda b, c: (b, c, 0))] + [full(a) for a in params],
        out_specs=pl.BlockSpec((nb, CHUNK, RW_WIDTH), lambda b, c: (b, c, 0)),
        out_shape=jax.ShapeDtypeStruct((bsz, t_len, RW_WIDTH), F32),
        scratch_shapes=[pltpu.VMEM((nb, 8, RW_COLS), F32), pltpu.VMEM((nb, RW_PAIRS, LANES, LANES), F32)],
        compiler_params=pltpu.CompilerParams(dimension_semantics=("parallel", "arbitrary"),
                                             vmem_limit_bytes=VMEM_LIMIT),
        name="rwkv7",
    )(z_rw, *params)


def _cummax_rows(x):
    rows = _iota(x.shape, 0)
    d = 1
    while d < x.shape[0]:
        x = jnp.maximum(x, jnp.where(rows < d, NEG_INF, pltpu.roll(x, d, 0)))
        d *= 2
    return x


def _dot_exact_rhs3(a, b_bf16):
    hi = a.astype(BF16)
    rem = a - hi.astype(F32)
    mid = rem.astype(BF16)
    lo = (rem - mid.astype(F32)).astype(BF16)
    return _dot(hi, b_bf16) + _dot(mid, b_bf16) + _dot(lo, b_bf16)


def _mlstm_kernel(z_ref, zif_ref, cw_ref, cb_ref, bif_ref, nw_ref, h_ref, conv_ref, c_ref, n_ref, m_ref):
    c = pl.program_id(1)
    nb = z_ref.shape[0]

    @pl.when(c == 0)
    def _():
        conv_ref[:, 0:8, :] = jnp.zeros((nb, 8, 2 * ML_WIDTH), F32)
        c_ref[...] = jnp.zeros_like(c_ref)
        n_ref[...] = jnp.zeros_like(n_ref)
        m_ref[...] = jnp.zeros_like(m_ref)

    tri = (_iota((CHUNK, CHUNK), 1) <= _iota((CHUNK, CHUNK), 0))
    tri_b = tri.astype(BF16)
    ones_sq = jnp.ones((LANES, LANES), BF16)
    ones_tall = jnp.ones((CHUNK, LANES), BF16)
    src = _iota((LANES, 2 * ML_HEADS * LANES), 0)
    blk_id = _iota((LANES, 2 * ML_HEADS * LANES), 1) // LANES
    head = blk_id // 2
    is_diff = (blk_id % 2) == 0
    sel = jnp.where(src == head + ML_HEADS, jnp.where(is_diff, -1.0, 1.0),
                    jnp.where(jnp.logical_and(is_diff, src == head), 1.0, 0.0)).astype(BF16)
    gate_lane = _iota((CHUNK, LANES), 1)

    items = []
    for b in range(nb):
        conv_ref[b, 8:8 + CHUNK, :] = z_ref[b, :, 0:2 * ML_WIDTH]
        acc = jnp.broadcast_to(cb_ref[...], (CHUNK, 2 * ML_WIDTH))
        for j in range(ML_CONV):
            off = 8 - (ML_CONV - 1) + j
            acc = acc + cw_ref[j:j + 1, :] * conv_ref[b, off:off + CHUNK, :]
        conv_ref[b, 0:8, :] = conv_ref[b, CHUNK:CHUNK + 8, :]
        qk = acc * _sigmoid(acc)
        q = qk[:, :ML_WIDTH]
        k = qk[:, ML_WIDTH:] * (ML_HEAD_DIM ** -0.5)
        v = z_ref[b, :, 2 * ML_WIDTH:3 * ML_WIDTH]
        o_pre = z_ref[b, :, 3 * ML_WIDTH:4 * ML_WIDTH]

        gz = zif_ref[b] + bif_ref[...]
        lf = -_softplus(-gz)
        bcum = _dot_exact_lhs(tri_b, lf)
        gates = jnp.where(gate_lane < ML_HEADS, gz, bcum)
        col_b = _dot_exact_rhs3(gates, sel)
        gates_t = gates.T
        for h in range(ML_HEADS):
            sl = slice(h * ML_HEAD_DIM, (h + 1) * ML_HEAD_DIM)
            items.append(dict(b=b, h=h, sl=sl, q=q[:, sl], k=k[:, sl], v=v[:, sl], o=o_pre[:, sl],
                              diff_col=col_b[:, 2 * h * LANES:(2 * h + 1) * LANES],
                              bc_col=col_b[:, (2 * h + 1) * LANES:(2 * h + 2) * LANES],
                              diff_row=gates_t[h:h + 1, :] - gates_t[ML_HEADS + h:ML_HEADS + h + 1, :]))

    def each(fn):
        return [fn(it) for it in items]

    def put(name, vals):
        for it, val in zip(items, vals):
            it[name] = val

    for it in items:
        it["qb"] = it["q"].astype(BF16)
        it["kb"] = it["k"].astype(BF16)
        it["c_in"] = c_ref[it["b"], it["h"]]
        it["n_in"] = n_ref[it["b"], it["h"], 0:1, :]
        it["m_in"] = m_ref[it["b"], it["h"], 0:1, :]
        it["g_end"] = it["bc_col"][CHUNK - 1:CHUNK, :]
        it["m_loc"] = jnp.max(it["diff_col"], axis=0, keepdims=True) + it["g_end"]
        wts = jnp.exp(it["diff_col"] + (it["g_end"] - it["m_loc"]))
        it["k_w"] = it["k"] * wts
        it["m_t"] = it["bc_col"] + jnp.maximum(_cummax_rows(it["diff_col"]), it["m_in"])
        d_log = jnp.where(tri, it["bc_col"][:, :CHUNK] + it["diff_row"], NEG_INF)
        it["p"] = jnp.exp(d_log - it["m_t"][:, :CHUNK])
        it["inter_w"] = jnp.exp(it["bc_col"] + it["m_in"] - it["m_t"])
        it["v_ext"] = jnp.concatenate([it["v"].astype(BF16), ones_tall], axis=1)
        n_rep = jnp.broadcast_to(it["n_in"], (ML_HEAD_DIM, ML_HEAD_DIM))
        it["c_ext"] = jnp.concatenate([it["c_in"], n_rep], axis=0).astype(BF16)
    put("qk", each(lambda it: _dot_nt(it["qb"], it["kb"])))
    put("qc", each(lambda it: _dot_nt(it["qb"], it["c_ext"])))
    put("c_loc", each(lambda it: _dot_tn(it["v_ext"][:, :ML_HEAD_DIM], it["k_w"].astype(BF16))))
    put("sv", each(lambda it: _dot((it["qk"] * it["p"]).astype(BF16), it["v_ext"])))
    for it in items:
        d = ML_HEAD_DIM
        num = it["sv"][:, :d] + it["inter_w"] * it["qc"][:, :d]
        den = it["sv"][:, d:] + it["inter_w"] * it["qc"][:, d:]
        it["hh"] = num / jnp.maximum(jnp.abs(den), jnp.exp(-it["m_t"]))
    put("mean", each(lambda it: _dot_exact_rhs(it["hh"], ones_sq) * (1.0 / ML_HEAD_DIM)))
    put("hc", each(lambda it: it["hh"] - it["mean"]))
    put("var", each(lambda it: _dot_exact_rhs(it["hc"] * it["hc"], ones_sq) * (1.0 / ML_HEAD_DIM)))
    for it in items:
        hn = it["hc"] * lax.rsqrt(it["var"] + ML_NORM_EPS) * nw_ref[:, it["sl"]]
        h_ref[it["b"], :, it["sl"]] = hn * _sigmoid(it["o"])

        m_new = jnp.maximum(it["g_end"] + it["m_in"], it["m_loc"])
        s_old = jnp.exp(it["g_end"] + it["m_in"] - m_new)
        s_new = jnp.exp(it["m_loc"] - m_new)
        n_loc = jnp.sum(it["k_w"], axis=0, keepdims=True)
        c_ref[it["b"], it["h"]] = s_old * it["c_in"] + s_new * it["c_loc"]
        n_ref[it["b"], it["h"]] = jnp.broadcast_to(s_old * it["n_in"] + s_new * n_loc, (8, ML_HEAD_DIM))
        m_ref[it["b"], it["h"]] = jnp.broadcast_to(m_new, (8, LANES))


def _mlstm(z_ml, z_if, conv_w, conv_b, b_if, norm_w, nb=4):
    bsz, t_len, _ = z_ml.shape
    full = lambda a: pl.BlockSpec(a.shape, lambda b, c: (0,) * a.ndim)
    params = (conv_w, conv_b, b_if, norm_w)
    return pl.pallas_call(
        _mlstm_kernel,
        grid=(bsz // nb, t_len // CHUNK),
        in_specs=[pl.BlockSpec((nb, CHUNK, ML_MAIN), lambda b, c: (b, c, 0)),
                  pl.BlockSpec((nb, CHUNK, LANES), lambda b, c: (b, c, 0))] + [full(a) for a in params],
        out_specs=pl.BlockSpec((nb, CHUNK, ML_WIDTH), lambda b, c: (b, c, 0)),
        out_shape=jax.ShapeDtypeStruct((bsz, t_len, ML_WIDTH), F32),
        scratch_shapes=[pltpu.VMEM((nb, 8 + CHUNK, 2 * ML_WIDTH), F32),
                        pltpu.VMEM((nb, ML_HEADS, ML_HEAD_DIM, ML_HEAD_DIM), F32),
                        pltpu.VMEM((nb, ML_HEADS, 8, ML_HEAD_DIM), F32),
                        pltpu.VMEM((nb, ML_HEADS, 8, LANES), F32)],
        compiler_params=pltpu.CompilerParams(dimension_semantics=("parallel", "arbitrary"),
                                             vmem_limit_bytes=VMEM_LIMIT),
        name="mlstm",
    )(z_ml, z_if, *params)


def _merge_kernel(x_ref, yrw_ref, yml_ref, zg_ref, gb_ref, prw_ref, pml_ref, wout_ref, n2_ref, wq_ref,
                  x1_ref, tok_ref, q_ref):
    gate = _sigmoid(zg_ref[...] + gb_ref[...])
    m_rw = _dot(yrw_ref[...].astype(BF16), prw_ref[...])
    m_ml = _dot(yml_ref[...].astype(BF16), pml_ref[...])
    merged = gate[:, :D_MODEL] * m_rw + gate[:, D_MODEL:] * m_ml
    x1 = x_ref[...] + _dot(merged.astype(BF16), wout_ref[...])
    x1_ref[...] = x1
    tok = x1 * lax.rsqrt(jnp.mean(x1 * x1, axis=-1, keepdims=True) + RMS_EPS) * n2_ref[...]
    tok_b = tok.astype(BF16)
    tok_ref[...] = tok_b
    q_ref[...] = _dot(tok_b, wq_ref[...])


def _merge(x2, y_rw, y_ml, z_gate, gate_b, p_rw, p_ml, w_out, n2, w_q, tm=512):
    n = x2.shape[0]
    row = lambda w: pl.BlockSpec((tm, w), lambda i: (i, 0))
    full = lambda a: pl.BlockSpec(a.shape, lambda i: (0,) * a.ndim)
    qd = w_q.shape[1]
    return pl.pallas_call(
        _merge_kernel,
        grid=(n // tm,),
        in_specs=[row(D_MODEL), row(RW_WIDTH), row(ML_WIDTH), row(2 * D_MODEL), full(gate_b), full(p_rw),
                  full(p_ml), full(w_out), full(n2), full(w_q)],
        out_specs=[row(D_MODEL), row(D_MODEL), row(qd)],
        out_shape=[jax.ShapeDtypeStruct((n, D_MODEL), F32), jax.ShapeDtypeStruct((n, D_MODEL), BF16),
                   jax.ShapeDtypeStruct((n, qd), F32)],
        compiler_params=pltpu.CompilerParams(dimension_semantics=("parallel",), vmem_limit_bytes=VMEM_LIMIT),
        name="merge",
    )(x2, y_rw, y_ml, z_gate, gate_b, p_rw, p_ml, w_out, n2, w_q)


_CELLS = [(r0, r1) for r0 in range(PEER_TOPK) for r1 in range(PEER_TOPK) if (r0 + 1) * (r1 + 1) <= PEER_TOPK]
_CELL_ROWS = 56


def _top16_exact(s):
    kiota = _iota(s.shape, 0)
    rank = jnp.full(s.shape, float(PEER_KEYS - 1), F32)
    tops = []
    for r in range(PEER_TOPK):
        m = jnp.max(s, axis=0, keepdims=True)
        idx = jnp.min(jnp.where(s == m, kiota, PEER_KEYS), axis=0, keepdims=True)
        hit = kiota == idx
        rank = jnp.where(hit, float(r), rank)
        s = jnp.where(hit, NEG_INF, s)
        tops.append(m)
    return rank, jnp.concatenate(tops, axis=0)


def _top16_distinct(s):
    rank = jnp.full(s.shape, float(PEER_KEYS - 1), F32)
    tops = []
    for r in range(PEER_TOPK):
        m = jnp.max(s, axis=0, keepdims=True)
        hit = s == m
        rank = jnp.where(hit, float(r), rank)
        s = jnp.where(hit, NEG_INF, s)
        tops.append(m)
    return rank, jnp.concatenate(tops, axis=0)


_SORT16 = ((0, 1), (2, 3), (0, 2), (1, 3), (1, 2), (4, 5), (6, 7), (4, 6), (5, 7), (5, 6), (0, 4), (2, 6), (2, 4),
           (1, 5), (3, 7), (3, 5), (1, 2), (3, 4), (5, 6), (8, 9), (10, 11), (8, 10), (9, 11), (9, 10), (12, 13),
           (14, 15), (12, 14), (13, 15), (13, 14), (8, 12), (10, 14), (10, 12), (9, 13), (11, 15), (11, 13), (9, 10),
           (11, 12), (13, 14), (0, 8), (4, 12), (4, 8), (2, 10), (6, 14), (6, 10), (2, 4), (6, 8), (10, 12), (1, 9),
           (5, 13), (5, 9), (3, 11), (7, 15), (7, 11), (3, 5), (7, 9), (11, 13), (1, 2), (3, 4), (5, 6), (7, 8),
           (9, 10), (11, 12), (13, 14))


def _top16_values(s):
    n_tiles = PEER_KEYS // 8
    v = [s[8 * p:8 * (p + 1), :] for p in range(n_tiles)]
    for i, j in _SORT16:
        v[i], v[j] = jnp.maximum(v[i], v[j]), jnp.minimum(v[i], v[j])
    tops = []
    popped = jnp.zeros(v[0].shape, F32)
    for r in range(PEER_TOPK):
        m = jnp.max(v[0], axis=0, keepdims=True)
        hit = v[0] == m
        popped = popped + jnp.where(hit, 1.0, 0.0)
        tops.append(m)
        depth = PEER_TOPK - 1 - r
        for p in range(depth):
            v[p] = jnp.where(hit, v[p + 1], v[p])
    n_pop = jnp.sum(popped, axis=0, keepdims=True)
    for r in range(PEER_TOPK - 1):
        n_pop = n_pop + jnp.where(tops[r] == tops[r + 1], 1.0, 0.0)
    return jnp.concatenate(tops, axis=0), n_pop


def _select_kernel(q_ref, sk_ref, a_ref, l_ref, b_ref, c_ref, s_ref, rank_ref, tops_ref, ktops_ref):
    tn = q_ref.shape[0]
    groups = [(h, half) for h in range(PEER_HEADS) for half in range(2)]

    n_ranked = []
    for g, (h, half) in enumerate(groups):
        base = h * PEER_QDIM + half * PEER_HALF
        s = _dot_nt(sk_ref[h, half], q_ref[:, base:base + PEER_HALF])
        s_ref[g] = s
        if half == 0:
            tops, n_top = _top16_values(s)
            rank_ref[g] = s
            ktops_ref[h] = tops
        else:
            rank, tops = _top16_distinct(s)
            rank_ref[g] = rank
            n_top = jnp.sum((rank < float(PEER_TOPK)).astype(F32), axis=0, keepdims=True)
        tops_ref[g] = tops
        n_ranked.append(n_top)
    most_ranked = functools.reduce(jnp.maximum, n_ranked)

    @pl.when(jnp.max(most_ranked) > float(PEER_TOPK))
    def _():
        for g, (h, half) in enumerate(groups):
            @pl.when(jnp.max(n_ranked[g]) > float(PEER_TOPK))
            def _():
                rank_x, tops_x = _top16_exact(s_ref[g])
                tops_ref[g] = tops_x
                if half == 0:
                    rank_ref[g] = -rank_x
                    ktops_ref[h] = -_iota((PEER_TOPK, tn), 0).astype(F32)
                else:
                    rank_ref[g] = rank_x

    pos_col = jnp.concatenate(
        [jnp.full((1, 1), r0 * PEER_TOPK + r1, jnp.int32) for r0, r1 in _CELLS]
        + [jnp.full((_CELL_ROWS - len(_CELLS), 1), 1 << 20, jnp.int32)], axis=0)
    pos = jnp.broadcast_to(pos_col, (_CELL_ROWS, tn))
    for h in range(PEER_HEADS):
        tops0 = tops_ref[2 * h]
        tops1 = tops_ref[2 * h + 1]
        cand = jnp.concatenate([tops0[r0:r0 + 1] + tops1[r1:r1 + 1] for r0, r1 in _CELLS]
                               + [jnp.full((_CELL_ROWS - len(_CELLS), tn), NEG_INF, F32)], axis=0)
        work = cand
        sel = jnp.zeros(cand.shape, jnp.bool_)
        for _ in range(PEER_TOPK):
            m = jnp.max(work, axis=0, keepdims=True)
            idx = jnp.min(jnp.where(work == m, pos, 1 << 20), axis=0, keepdims=True)
            hit = pos == idx
            sel = jnp.logical_or(sel, hit)
            work = jnp.where(hit, NEG_INF, work)
        cmax = tops0[0:1] + tops1[0:1]
        e_sel = jnp.where(sel, jnp.exp(cand - cmax), 0.0)
        z = jnp.sum(e_sel, axis=0, keepdims=True)
        self_f = sel.astype(F32)
        counts = []
        start = 0
        for r0 in range(PEER_TOPK):
            n_cells = PEER_TOPK // (r0 + 1)
            counts.append(jnp.sum(self_f[start:start + n_cells, :], axis=0, keepdims=True))
            start += n_cells

        key0 = rank_ref[2 * h]
        ktops = ktops_ref[h]
        cnt = jnp.zeros(key0.shape, F32)
        for r0 in range(PEER_TOPK):
            cnt = jnp.where(key0 == ktops[r0:r0 + 1], counts[r0], cnt)
        a_ref[h] = jnp.exp(s_ref[2 * h] - tops0[0:1]) / z
        l_ref[h] = cnt
        b_ref[h] = jnp.exp(s_ref[2 * h + 1] - tops1[0:1]).astype(BF16)
        c_ref[h] = rank_ref[2 * h + 1].astype(BF16)


def _peer_select(q, sub_keys, tn=256):
    n = q.shape[0]
    shape = (PEER_HEADS, PEER_KEYS, n)
    ospec = pl.BlockSpec((PEER_HEADS, PEER_KEYS, tn), lambda i: (0, 0, i))
    return pl.pallas_call(
        _select_kernel,
        grid=(n // tn,),
        in_specs=[pl.BlockSpec((tn, q.shape[1]), lambda i: (i, 0)),
                  pl.BlockSpec(sub_keys.shape, lambda i: (0, 0, 0, 0))],
        out_specs=[ospec] * 4,
        out_shape=[jax.ShapeDtypeStruct(shape, F32), jax.ShapeDtypeStruct(shape, F32),
                   jax.ShapeDtypeStruct(shape, BF16), jax.ShapeDtypeStruct(shape, BF16)],
        scratch_shapes=[pltpu.VMEM((2 * PEER_HEADS, PEER_KEYS, tn), F32),
                        pltpu.VMEM((2 * PEER_HEADS, PEER_KEYS, tn), F32),
                        pltpu.VMEM((2 * PEER_HEADS, PEER_TOPK, tn), F32),
                        pltpu.VMEM((PEER_HEADS, PEER_TOPK, tn), F32)],
        compiler_params=pltpu.CompilerParams(dimension_semantics=("parallel",), vmem_limit_bytes=VMEM_LIMIT),
        name="peer_select",
    )(q, sub_keys)


GATE_COLS = 256
ROW_SLAB = 8


def _gelu(x):
    return 0.5 * x * (1.0 + lax.erf(x * (1.0 / math.sqrt(2.0))))


def _gate_block(slab0, row0, il, cols, rows_ref, b_ref, c_ref, h_ref, m_ref):
    zero = jnp.zeros((PEER_KEYS, GATE_COLS), BF16)
    rows = slice(il * PEER_KEYS, (il + 1) * PEER_KEYS)
    r = row0 + il
    gate = zero
    for h in range(PEER_HEADS):
        a_b = jnp.broadcast_to(rows_ref[slab0 + h, r:r + 1, cols], zero.shape).astype(BF16)
        l_b = jnp.broadcast_to(rows_ref[slab0 + PEER_HEADS + h, r:r + 1, cols], zero.shape).astype(BF16)
        gate = gate + jnp.where(c_ref[h, :, cols] < l_b, b_ref[h, :, cols], zero) * a_b
    m_ref[rows, cols] = _gelu(h_ref[rows, cols].astype(BF16)) * gate


TILES_PER_STEP = 4


def _dense_kernel(tokt_ref, u_ref, vta_ref, vtb_ref, a_ref, l_ref, b_ref, c_ref, x1_ref, fg_ref, out_ref,
                  acc_ref, h0_ref, h1_ref, m0_ref, m1_ref, rows_ref):
    e = pl.program_id(1)
    te, tn = h0_ref.shape
    n_tiles = PEER_EXPERTS // te
    n_full = n_tiles // TILES_PER_STEP
    n_slabs = PEER_KEYS // ROW_SLAB
    rows_per_tile = te // PEER_KEYS
    assert 2 * rows_per_tile == ROW_SLAB and TILES_PER_STEP == 4
    h_bufs = (h0_ref, h1_ref)
    m_bufs = (m0_ref, m1_ref)

    slab_ids = (jnp.maximum(2 * e - 1, 0), jnp.minimum(2 * e, n_slabs - 1), jnp.minimum(2 * e + 1, n_slabs - 1))
    for s_idx, sid in enumerate(slab_ids):
        i0 = pl.multiple_of(ROW_SLAB * sid, ROW_SLAB)
        for h in range(PEER_HEADS):
            rows_ref[s_idx * 2 * PEER_HEADS + h] = a_ref[h, pl.ds(i0, ROW_SLAB), :]
            rows_ref[s_idx * 2 * PEER_HEADS + PEER_HEADS + h] = l_ref[h, pl.ds(i0, ROW_SLAB), :]

    @pl.when(e == 0)
    def _():
        acc_ref[...] = jnp.zeros_like(acc_ref)
        m0_ref[...] = jnp.zeros_like(m0_ref)
        h1_ref[...] = jnp.zeros_like(h1_ref)

    def gate_tile(j, par, cols):
        s_idx, row0 = ((0, rows_per_tile), (1, 0), (1, rows_per_tile), (2, 0))[j]
        for il in range(rows_per_tile):
            _gate_block(s_idx * 2 * PEER_HEADS, row0, il, cols, rows_ref, b_ref, c_ref, h_bufs[1 - par],
                        m_bufs[1 - par])

    def vt_tile(j):
        if j < 2:
            return vta_ref[:, j * te:(j + 1) * te]
        return vtb_ref[:, (j - 2) * te:(j - 1) * te]

    @pl.when(e < n_full)
    def _():
        for j in range(TILES_PER_STEP):
            par = j % 2
            for cg in range(tn // GATE_COLS):
                cols = slice(cg * GATE_COLS, (cg + 1) * GATE_COLS)
                acc_ref[:, cols] += _dot(vt_tile(j), m_bufs[par][:, cols])
                h_bufs[par][:, cols] = _dot(u_ref[j * te:(j + 1) * te, :], tokt_ref[:, cols])
                gate_tile(j, par, cols)

    @pl.when(e == n_full)
    def _():
        for cg in range(tn // GATE_COLS):
            cols = slice(cg * GATE_COLS, (cg + 1) * GATE_COLS)
            gate_tile(0, 0, cols)
            acc_ref[:, cols] += _dot(vt_tile(0), m0_ref[:, cols]) + _dot(vt_tile(1), m1_ref[:, cols])
        x2 = x1_ref[...] + acc_ref[...].T
        out_ref[...] = x2 * lax.rsqrt(jnp.mean(x2 * x2, axis=-1, keepdims=True) + RMS_EPS) * fg_ref[...]


def _peer_dense(tok_t, u_b, v_t, a, l, b, c, x1, final_g, tn=512, te=512):
    n = x1.shape[0]
    n_full = PEER_EXPERTS // (TILES_PER_STEP * te)
    n_pairs = PEER_EXPERTS // (2 * te)
    sel_spec = pl.BlockSpec((PEER_HEADS, PEER_KEYS, tn), lambda t, e: (0, 0, t))
    return pl.pallas_call(
        _dense_kernel,
        grid=(n // tn, n_full + 1),
        in_specs=[pl.BlockSpec((D_MODEL, tn), lambda t, e: (0, t)),
                  pl.BlockSpec((TILES_PER_STEP * te, D_MODEL), lambda t, e: (jnp.minimum(e, n_full - 1), 0)),
                  pl.BlockSpec((D_MODEL, 2 * te), lambda t, e: (0, jnp.maximum(2 * e - 1, 0))),
                  pl.BlockSpec((D_MODEL, 2 * te), lambda t, e: (0, jnp.minimum(2 * e, n_pairs - 1))),
                  sel_spec, sel_spec, sel_spec, sel_spec,
                  pl.BlockSpec((tn, D_MODEL), lambda t, e: (t, 0)),
                  pl.BlockSpec((1, D_MODEL), lambda t, e: (0, 0))],
        out_specs=pl.BlockSpec((tn, D_MODEL), lambda t, e: (t, 0)),
        out_shape=jax.ShapeDtypeStruct((n, D_MODEL), F32),
        scratch_shapes=[pltpu.VMEM((D_MODEL, tn), F32), pltpu.VMEM((te, tn), F32), pltpu.VMEM((te, tn), F32),
                        pltpu.VMEM((te, tn), BF16), pltpu.VMEM((te, tn), BF16),
                        pltpu.VMEM((6 * PEER_HEADS, ROW_SLAB, tn), F32)],
        compiler_params=pltpu.CompilerParams(dimension_semantics=("parallel", "arbitrary"),
                                             vmem_limit_bytes=VMEM_LIMIT),
        name="peer_dense",
    )(tok_t, u_b, v_t, v_t, a, l, b, c, x1, final_g)


def _row(v):
    return v.reshape(1, -1).astype(F32)


def kernel(x, norm1_g, w_in, rw_mu, rw_w0, rw_w_up, rw_a0, rw_a_up, rw_g_up, rw_k_k, rw_k_a, rw_r_k, rw_gn_w, rw_gn_b, ml_conv_q_w, ml_conv_q_b, ml_conv_k_w, ml_conv_k_b, ml_b_i, ml_b_f, ml_norm_w, gate_b, p_rw, p_ml, w_out, norm2_g, peer_w_q, peer_sub_keys, peer_u, peer_v, final_g):
    bsz, t_len, d = x.shape
    n = bsz * t_len
    x2 = x.reshape(n, d)
    l = 0

    w = w_in[l]
    ml0 = RW_COLS
    if0 = RW_COLS + ML_MAIN
    gt0 = if0 + 2 * ML_HEADS
    wrw = w[:, :ml0].astype(BF16)
    wml = w[:, ml0:if0].astype(BF16)
    wif = jnp.pad(w[:, if0:gt0], ((0, 0), (0, LANES - 2 * ML_HEADS))).astype(BF16)
    wgt = w[:, gt0:].astype(BF16)
    z_rw, z_ml, z_if, z_gate = _inproj(x2, _row(norm1_g[l]), wrw, wml, wif, wgt)

    zero_lora = jnp.zeros((RW_DECAY_LORA, RW_WIDTH), F32)
    wup = jnp.concatenate([rw_w_up[l], zero_lora], axis=0).astype(BF16)
    aup = jnp.concatenate([zero_lora, rw_a_up[l]], axis=0).astype(BF16)
    y_rw = _rwkv(z_rw.reshape(bsz, t_len, RW_COLS), _row(rw_mu[l]), _row(rw_w0[l]), wup, _row(rw_a0[l]), aup,
                 rw_g_up[l].astype(BF16), _row(rw_k_k[l]), _row(rw_k_a[l]), _row(rw_r_k[l]), _row(rw_gn_w[l]),
                 _row(rw_gn_b[l]))

    conv_w = jnp.concatenate([ml_conv_q_w[l], ml_conv_k_w[l]], axis=1)
    conv_b = jnp.concatenate([ml_conv_q_b[l], ml_conv_k_b[l]]).reshape(1, -1)
    b_if = jnp.pad(jnp.concatenate([ml_b_i[l], ml_b_f[l]]), (0, LANES - 2 * ML_HEADS)).reshape(1, -1)
    y_ml = _mlstm(z_ml.reshape(bsz, t_len, ML_MAIN), z_if.reshape(bsz, t_len, LANES), conv_w, conv_b, b_if,
                  _row(ml_norm_w[l]))

    x1, tok, q = _merge(x2, y_rw.reshape(n, RW_WIDTH), y_ml.reshape(n, ML_WIDTH), z_gate, _row(gate_b[l]),
                        p_rw[l].astype(BF16), p_ml[l].astype(BF16), w_out[l].astype(BF16), _row(norm2_g[l]),
                        peer_w_q[l].astype(BF16))

    a, cnt, b, c = _peer_select(q, peer_sub_keys[l])
    out = _peer_dense(tok.T, peer_u[l].astype(BF16), peer_v[l].T.astype(BF16), a, cnt, b, c, x1, _row(final_g))
    return out.reshape(bsz, t_len, d)
```

```python
import functools
import math

import jax
import jax.numpy as jnp
from jax import lax
from jax.experimental import pallas as pl
from jax.experimental.pallas import tpu as pltpu

F32 = jnp.float32
BF16 = jnp.bfloat16

D_MODEL = 1024
CHUNK = 64
RMS_EPS = 1e-6
RW_HEADS = 8
RW_HEAD_DIM = 64
RW_WIDTH = 512
RW_DECAY_LORA = 64
RW_AAA_LORA = 64
RW_GATE_LORA = 128
RW_GN_EPS = 64e-5
RW_COLS = 3 * RW_WIDTH + RW_DECAY_LORA + RW_AAA_LORA + RW_GATE_LORA
RW_PAIRS = RW_HEADS // 2
ML_HEADS = 4
ML_HEAD_DIM = 128
ML_WIDTH = 512
ML_CONV = 4
ML_NORM_EPS = 1e-5
ML_MAIN = 4 * ML_WIDTH
PEER_HEADS = 8
PEER_KEYS = 128
PEER_EXPERTS = PEER_KEYS * PEER_KEYS
PEER_QDIM = 256
PEER_HALF = 128
PEER_TOPK = 16

LANES = 128
VMEM_LIMIT = 56 * 1024 * 1024

NEG_INF = float("-inf")


def _dot(a, b):
    return jnp.dot(a, b, preferred_element_type=F32)


def _dot_nt(a, b):
    return lax.dot_general(a, b, (((1,), (1,)), ((), ())), preferred_element_type=F32)


def _dot_tn(a, b):
    return lax.dot_general(a, b, (((0,), (0,)), ((), ())), preferred_element_type=F32)


def _sigmoid(x):
    return 1.0 / (1.0 + jnp.exp(-x))


def _softplus(x):
    return jnp.maximum(x, 0.0) + jnp.log(1.0 + jnp.exp(-jnp.abs(x)))


def _iota(shape, dim):
    return lax.broadcasted_iota(jnp.int32, shape, dim)


def _split2(x):
    hi = x.astype(BF16)
    return hi, (x - hi.astype(F32)).astype(BF16)


def _dot_exact_rhs(a, b_bf16):
    hi, lo = _split2(a)
    return _dot(hi, b_bf16) + _dot(lo, b_bf16)


def _dot_exact_lhs(a_bf16, b):
    hi = b.astype(BF16)
    rem = b - hi.astype(F32)
    mid = rem.astype(BF16)
    lo = (rem - mid.astype(F32)).astype(BF16)
    return _dot(a_bf16, hi) + _dot(a_bf16, mid) + _dot(a_bf16, lo)


def _inproj_kernel(x_ref, g_ref, wrw_ref, wml_ref, wif_ref, wgt_ref, zrw_ref, zml_ref, zif_ref, zgt_ref):
    x = x_ref[...]
    h = x * lax.rsqrt(jnp.mean(x * x, axis=-1, keepdims=True) + RMS_EPS) * g_ref[...]
    hb = h.astype(BF16)
    zrw_ref[...] = _dot(hb, wrw_ref[...])
    zml_ref[...] = _dot(hb, wml_ref[...])
    zif_ref[...] = _dot(hb, wif_ref[...])
    zgt_ref[...] = _dot(hb, wgt_ref[...])


def _inproj(x2, g, wrw, wml, wif, wgt, tm=512):
    n = x2.shape[0]
    row = lambda w: pl.BlockSpec((tm, w), lambda i: (i, 0))
    full = lambda a: pl.BlockSpec(a.shape, lambda i: (0,) * a.ndim)
    return pl.pallas_call(
        _inproj_kernel,
        grid=(n // tm,),
        in_specs=[row(D_MODEL), full(g), full(wrw), full(wml), full(wif), full(wgt)],
        out_specs=[row(wrw.shape[1]), row(wml.shape[1]), row(wif.shape[1]), row(wgt.shape[1])],
        out_shape=[jax.ShapeDtypeStruct((n, w.shape[1]), F32) for w in (wrw, wml, wif, wgt)],
        compiler_params=pltpu.CompilerParams(dimension_semantics=("parallel",), vmem_limit_bytes=VMEM_LIMIT),
        name="inproj",
    )(x2, g, wrw, wml, wif, wgt)


def _bd(xw):
    xb = xw.astype(BF16)
    left = _iota(xb.shape, 1) < RW_HEAD_DIM
    zero = jnp.zeros_like(xb)
    return jnp.concatenate([jnp.where(left, xb, zero), jnp.where(left, zero, xb)], axis=0)


def _mm(xw, yw):
    return _dot(xw.astype(BF16), _bd(yw))


def _rwkv_kernel(z_ref, mu_ref, w0_ref, wup_ref, a0_ref, aup_ref, gup_ref, kk_ref, ka_ref, rk_ref,
                 gnw_ref, gnb_ref, y_ref, prev_ref, st_ref):
    c = pl.program_id(1)
    nb = z_ref.shape[0]

    @pl.when(c == 0)
    def _():
        prev_ref[...] = jnp.zeros_like(prev_ref)
        st_ref[...] = jnp.zeros_like(st_ref)

    ii = _iota((CHUNK, LANES), 0)
    jj = _iota((CHUNK, LANES), 1) % RW_HEAD_DIM
    strict = jj < ii
    incl = jj <= ii
    eye_w = (jj == ii).astype(F32)
    same16 = (ii // 16) == (jj // 16)
    same32 = (ii // 32) == (jj // 32)
    tri = (_iota((CHUNK, CHUNK), 1) <= _iota((CHUNK, CHUNK), 0)).astype(BF16)
    r2 = _iota((LANES, LANES), 0)
    c2 = _iota((LANES, LANES), 1)
    blk = (r2 // RW_HEAD_DIM) == (c2 // RW_HEAD_DIM)
    ones_bd = blk.astype(BF16)
    eye2 = r2 == c2
    first = _iota((CHUNK, RW_COLS), 0) == 0
    o1, o2, o3 = RW_WIDTH, 2 * RW_WIDTH, 3 * RW_WIDTH

    items = []
    for b in range(nb):
        z = z_ref[b]
        z_prev = jnp.where(first, jnp.broadcast_to(prev_ref[b, 0:1, :], z.shape), pltpu.roll(z, 1, 0))
        prev_ref[b, 0:1, :] = z[CHUNK - 1:CHUNK, :]
        zs = z + (z_prev - z) * mu_ref[...]
        r = zs[:, :o1]
        k = zs[:, o1:o2]
        v = zs[:, o2:o3]
        zwa = zs[:, o3:o3 + LANES]
        zg = zs[:, o3 + LANES:]
        w_log = -_softplus(-(w0_ref[...] + _dot(jnp.tanh(zwa).astype(BF16), wup_ref[...]))) - 0.5
        logd = -jnp.exp(w_log)
        a = _sigmoid(a0_ref[...] + _dot(zwa.astype(BF16), aup_ref[...]))
        g = _dot(_sigmoid(zg).astype(BF16), gup_ref[...])
        kk_raw = k * kk_ref[...]
        k = k * (1.0 + (a - 1.0) * ka_ref[...])
        rkr = r * k * rk_ref[...]
        for p in range(RW_PAIRS):
            sl = slice(p * LANES, (p + 1) * LANES)
            items.append(dict(b=b, p=p, sl=sl, r=r[:, sl], k=k[:, sl], v=v[:, sl], a=a[:, sl], ld=logd[:, sl],
                              kk=kk_raw[:, sl], rkr=rkr[:, sl], g=g[:, sl]))

    def each(fn):
        return [fn(it) for it in items]

    def put(name, vals):
        for it, val in zip(items, vals):
            it[name] = val

    put("ss", each(lambda it: _dot_exact_rhs(it["kk"] * it["kk"], ones_bd)))
    put("lcum", each(lambda it: _dot_exact_lhs(tri, it["ld"])))
    put("bonus", each(lambda it: _dot_exact_rhs(it["rkr"], ones_bd)))
    for it in items:
        kk_n = it["kk"] / jnp.maximum(jnp.sqrt(it["ss"]), 1e-12)
        lcum = it["lcum"]
        l_end = lcum[CHUNK - 1:CHUNK, :]
        p_inv = jnp.exp(-lcum)
        to_end = jnp.exp(l_end - lcum)
        beta = kk_n * it["a"]
        it["l_end"] = l_end
        it["rt"] = it["r"] * jnp.exp(lcum)
        it["kt"] = kk_n * jnp.exp(lcum - it["ld"])
        it["kh_end"] = it["k"] * to_end
        it["bh_end"] = beta * to_end
        it["lhs"] = jnp.concatenate([it["kt"], it["rt"]], axis=0).astype(BF16)
        it["rhs"] = jnp.concatenate([_bd(it["k"] * p_inv), _bd(beta * p_inv)], axis=0)
    put("sc", each(lambda it: _dot_nt(it["lhs"], it["rhs"])))
    for it in items:
        sc = it["sc"]
        it["a_vk"] = jnp.where(strict, sc[:CHUNK, :LANES], 0.0)
        it["n_w"] = jnp.where(strict, sc[:CHUNK, LANES:], 0.0)
        it["a_rk"] = jnp.where(incl, sc[CHUNK:, :LANES], 0.0)
        it["a_rb"] = jnp.where(incl, sc[CHUNK:, LANES:], 0.0)
        it["d1"] = jnp.where(same16, it["n_w"], 0.0)

    put("av", each(lambda it: _mm(it["a_vk"], it["v"])))
    put("d2", each(lambda it: _mm(it["d1"], it["d1"])))
    put("d4", each(lambda it: _mm(it["d2"], it["d2"])))
    put("t", each(lambda it: _mm(eye_w - it["d1"], eye_w + it["d2"])))
    put("d8", each(lambda it: _mm(it["d4"], it["d4"])))
    put("t", each(lambda it: _mm(it["t"], eye_w + it["d4"])))
    put("t", each(lambda it: _mm(it["t"], eye_w + it["d8"])))
    off32_mask = jnp.logical_and(same32, jnp.logical_not(same16))
    put("tmp", each(lambda it: _mm(it["t"], jnp.where(off32_mask, it["n_w"], 0.0))))
    put("t", each(lambda it: it["t"] - _mm(it["tmp"], it["t"])))
    put("tmp", each(lambda it: _mm(it["t"], jnp.where(same32, 0.0, it["n_w"]))))
    put("t", each(lambda it: it["t"] - _mm(it["tmp"], it["t"])))

    put("x_w", each(lambda it: _mm(it["t"], it["av"])))
    put("w_w", each(lambda it: _mm(it["t"], it["kt"])))
    put("st", each(lambda it: st_ref[it["b"], it["p"]]))
    put("st_b", each(lambda it: it["st"].astype(BF16)))
    put("u", each(lambda it: -(_dot(it["w_w"].astype(BF16), it["st_b"]) + it["x_w"])))
    put("y", each(lambda it: _dot(it["rt"].astype(BF16), it["st_b"]) + _mm(it["a_rk"], it["v"])
                  + _mm(it["a_rb"], it["u"])))
    put("d_st", each(lambda it: _dot_tn(jnp.concatenate([it["kh_end"], it["bh_end"]], axis=0).astype(BF16),
                                        jnp.concatenate([it["v"], it["u"]], axis=0).astype(BF16))))
    for it in items:
        p_col = jnp.sum(jnp.where(eye2, jnp.broadcast_to(jnp.exp(it["l_end"]), (LANES, LANES)), 0.0),
                        axis=1, keepdims=True)
        st_ref[it["b"], it["p"]] = it["st"] * p_col + jnp.where(blk, it["d_st"], 0.0)

    put("mean", each(lambda it: _dot_exact_rhs(it["y"], ones_bd) * (1.0 / RW_HEAD_DIM)))
    put("yc", each(lambda it: it["y"] - it["mean"]))
    put("var", each(lambda it: _dot_exact_rhs(it["yc"] * it["yc"], ones_bd) * (1.0 / RW_HEAD_DIM)))
    for it in items:
        yn = it["yc"] * lax.rsqrt(it["var"] + RW_GN_EPS) * gnw_ref[:, it["sl"]] + gnb_ref[:, it["sl"]]
        y_ref[it["b"], :, it["sl"]] = (yn + it["bonus"] * it["v"]) * it["g"]


def _rwkv(z_rw, mu, w0, wup, a0, aup, gup, k_k, k_a, r_k, gn_w, gn_b, nb=4):
    bsz, t_len, _ = z_rw.shape
    full = lambda a: pl.BlockSpec(a.shape, lambda b, c: (0,) * a.ndim)
    params = (mu, w0, wup, a0, aup, gup, k_k, k_a, r_k, gn_w, gn_b)
    return pl.pallas_call(
        _rwkv_kernel,
        grid=(bsz // nb, t_len // CHUNK),
        in_specs=[pl.BlockSpec((nb, CHUNK, RW_COLS), lambda b, c: (b, c, 0))] + [full(a) for a in params],
        out_specs=pl.BlockSpec((nb, CHUNK, RW_WIDTH), lambda b, c: (b, c, 0)),
        out_shape=jax.ShapeDtypeStruct((bsz, t_len, RW_WIDTH), F32),
        scratch_shapes=[pltpu.VMEM((nb, 8, RW_COLS), F32), pltpu.VMEM((nb, RW_PAIRS, LANES, LANES), F32)],
        compiler_params=pltpu.CompilerParams(dimension_semantics=("parallel", "arbitrary"),
                                             vmem_limit_bytes=VMEM_LIMIT),
        name="rwkv7",
    )(z_rw, *params)


def _cummax_rows(x):
    rows = _iota(x.shape, 0)
    d = 1
    while d < x.shape[0]:
        x = jnp.maximum(x, jnp.where(rows < d, NEG_INF, pltpu.roll(x, d, 0)))
        d *= 2
    return x


def _dot_exact_rhs3(a, b_bf16):
    hi = a.astype(BF16)
    rem = a - hi.astype(F32)
    mid = rem.astype(BF16)
    lo = (rem - mid.astype(F32)).astype(BF16)
    return _dot(hi, b_bf16) + _dot(mid, b_bf16) + _dot(lo, b_bf16)


def _mlstm_kernel(z_ref, zif_ref, cw_ref, cb_ref, bif_ref, nw_ref, h_ref, conv_ref, c_ref, n_ref, m_ref):
    c = pl.program_id(1)
    nb = z_ref.shape[0]

    @pl.when(c == 0)
    def _():
        conv_ref[:, 0:8, :] = jnp.zeros((nb, 8, 2 * ML_WIDTH), F32)
        c_ref[...] = jnp.zeros_like(c_ref)
        n_ref[...] = jnp.zeros_like(n_ref)
        m_ref[...] = jnp.zeros_like(m_ref)

    tri = (_iota((CHUNK, CHUNK), 1) <= _iota((CHUNK, CHUNK), 0))
    tri_b = tri.astype(BF16)
    ones_sq = jnp.ones((LANES, LANES), BF16)
    ones_tall = jnp.ones((CHUNK, LANES), BF16)
    src = _iota((LANES, 2 * ML_HEADS * LANES), 0)
    blk_id = _iota((LANES, 2 * ML_HEADS * LANES), 1) // LANES
    head = blk_id // 2
    is_diff = (blk_id % 2) == 0
    sel = jnp.where(src == head + ML_HEADS, jnp.where(is_diff, -1.0, 1.0),
                    jnp.where(jnp.logical_and(is_diff, src == head), 1.0, 0.0)).astype(BF16)
    gate_lane = _iota((CHUNK, LANES), 1)

    items = []
    for b in range(nb):
        conv_ref[b, 8:8 + CHUNK, :] = z_ref[b, :, 0:2 * ML_WIDTH]
        acc = jnp.broadcast_to(cb_ref[...], (CHUNK, 2 * ML_WIDTH))
        for j in range(ML_CONV):
            off = 8 - (ML_CONV - 1) + j
            acc = acc + cw_ref[j:j + 1, :] * conv_ref[b, off:off + CHUNK, :]
        conv_ref[b, 0:8, :] = conv_ref[b, CHUNK:CHUNK + 8, :]
        qk = acc * _sigmoid(acc)
        q = qk[:, :ML_WIDTH]
        k = qk[:, ML_WIDTH:] * (ML_HEAD_DIM ** -0.5)
        v = z_ref[b, :, 2 * ML_WIDTH:3 * ML_WIDTH]
        o_pre = z_ref[b, :, 3 * ML_WIDTH:4 * ML_WIDTH]

        gz = zif_ref[b] + bif_ref[...]
        lf = -_softplus(-gz)
        bcum = _dot_exact_lhs(tri_b, lf)
        gates = jnp.where(gate_lane < ML_HEADS, gz, bcum)
        col_b = _dot_exact_rhs3(gates, sel)
        gates_t = gates.T
        for h in range(ML_HEADS):
            sl = slice(h * ML_HEAD_DIM, (h + 1) * ML_HEAD_DIM)
            items.append(dict(b=b, h=h, sl=sl, q=q[:, sl], k=k[:, sl], v=v[:, sl], o=o_pre[:, sl],
                              diff_col=col_b[:, 2 * h * LANES:(2 * h + 1) * LANES],
                              bc_col=col_b[:, (2 * h + 1) * LANES:(2 * h + 2) * LANES],
                              diff_row=gates_t[h:h + 1, :] - gates_t[ML_HEADS + h:ML_HEADS + h + 1, :]))

    def each(fn):
        return [fn(it) for it in items]

    def put(name, vals):
        for it, val in zip(items, vals):
            it[name] = val

    for it in items:
        it["qb"] = it["q"].astype(BF16)
        it["kb"] = it["k"].astype(BF16)
        it["c_in"] = c_ref[it["b"], it["h"]]
        it["n_in"] = n_ref[it["b"], it["h"], 0:1, :]
        it["m_in"] = m_ref[it["b"], it["h"], 0:1, :]
        it["g_end"] = it["bc_col"][CHUNK - 1:CHUNK, :]
        it["m_loc"] = jnp.max(it["diff_col"], axis=0, keepdims=True) + it["g_end"]
        wts = jnp.exp(it["diff_col"] + (it["g_end"] - it["m_loc"]))
        it["k_w"] = it["k"] * wts
        it["m_t"] = it["bc_col"] + jnp.maximum(_cummax_rows(it["diff_col"]), it["m_in"])
        d_log = jnp.where(tri, it["bc_col"][:, :CHUNK] + it["diff_row"], NEG_INF)
        it["p"] = jnp.exp(d_log - it["m_t"][:, :CHUNK])
        it["inter_w"] = jnp.exp(it["bc_col"] + it["m_in"] - it["m_t"])
        it["v_ext"] = jnp.concatenate([it["v"].astype(BF16), ones_tall], axis=1)
        n_rep = jnp.broadcast_to(it["n_in"], (ML_HEAD_DIM, ML_HEAD_DIM))
        it["c_ext"] = jnp.concatenate([it["c_in"], n_rep], axis=0).astype(BF16)
    put("qk", each(lambda it: _dot_nt(it["qb"], it["kb"])))
    put("qc", each(lambda it: _dot_nt(it["qb"], it["c_ext"])))
    put("c_loc", each(lambda it: _dot_tn(it["v_ext"][:, :ML_HEAD_DIM], it["k_w"].astype(BF16))))
    put("sv", each(lambda it: _dot((it["qk"] * it["p"]).astype(BF16), it["v_ext"])))
    for it in items:
        d = ML_HEAD_DIM
        num = it["sv"][:, :d] + it["inter_w"] * it["qc"][:, :d]
        den = it["sv"][:, d:] + it["inter_w"] * it["qc"][:, d:]
        it["hh"] = num / jnp.maximum(jnp.abs(den), jnp.exp(-it["m_t"]))
    put("mean", each(lambda it: _dot_exact_rhs(it["hh"], ones_sq) * (1.0 / ML_HEAD_DIM)))
    put("hc", each(lambda it: it["hh"] - it["mean"]))
    put("var", each(lambda it: _dot_exact_rhs(it["hc"] * it["hc"], ones_sq) * (1.0 / ML_HEAD_DIM)))
    for it in items:
        hn = it["hc"] * lax.rsqrt(it["var"] + ML_NORM_EPS) * nw_ref[:, it["sl"]]
        h_ref[it["b"], :, it["sl"]] = hn * _sigmoid(it["o"])

        m_new = jnp.maximum(it["g_end"] + it["m_in"], it["m_loc"])
        s_old = jnp.exp(it["g_end"] + it["m_in"] - m_new)
        s_new = jnp.exp(it["m_loc"] - m_new)
        n_loc = jnp.sum(it["k_w"], axis=0, keepdims=True)
        c_ref[it["b"], it["h"]] = s_old * it["c_in"] + s_new * it["c_loc"]
        n_ref[it["b"], it["h"]] = jnp.broadcast_to(s_old * it["n_in"] + s_new * n_loc, (8, ML_HEAD_DIM))
        m_ref[it["b"], it["h"]] = jnp.broadcast_to(m_new, (8, LANES))


def _mlstm(z_ml, z_if, conv_w, conv_b, b_if, norm_w, nb=4):
    bsz, t_len, _ = z_ml.shape
    full = lambda a: pl.BlockSpec(a.shape, lambda b, c: (0,) * a.ndim)
    params = (conv_w, conv_b, b_if, norm_w)
    return pl.pallas_call(
        _mlstm_kernel,
        grid=(bsz // nb, t_len // CHUNK),
        in_specs=[pl.BlockSpec((nb, CHUNK, ML_MAIN), lambda b, c: (b, c, 0)),
                  pl.BlockSpec((nb, CHUNK, LANES), lambda b, c: (b, c, 0))] + [full(a) for a in params],
        out_specs=pl.BlockSpec((nb, CHUNK, ML_WIDTH), lambda b, c: (b, c, 0)),
        out_shape=jax.ShapeDtypeStruct((bsz, t_len, ML_WIDTH), F32),
        scratch_shapes=[pltpu.VMEM((nb, 8 + CHUNK, 2 * ML_WIDTH), F32),
                        pltpu.VMEM((nb, ML_HEADS, ML_HEAD_DIM, ML_HEAD_DIM), F32),
                        pltpu.VMEM((nb, ML_HEADS, 8, ML_HEAD_DIM), F32),
                        pltpu.VMEM((nb, ML_HEADS, 8, LANES), F32)],
        compiler_params=pltpu.CompilerParams(dimension_semantics=("parallel", "arbitrary"),
                                             vmem_limit_bytes=VMEM_LIMIT),
        name="mlstm",
    )(z_ml, z_if, *params)


def _merge_kernel(x_ref, yrw_ref, yml_ref, zg_ref, gb_ref, prw_ref, pml_ref, wout_ref, n2_ref, wq_ref,
                  x1_ref, tok_ref, q_ref):
    gate = _sigmoid(zg_ref[...] + gb_ref[...])
    m_rw = _dot(yrw_ref[...].astype(BF16), prw_ref[...])
    m_ml = _dot(yml_ref[...].astype(BF16), pml_ref[...])
    merged = gate[:, :D_MODEL] * m_rw + gate[:, D_MODEL:] * m_ml
    x1 = x_ref[...] + _dot(merged.astype(BF16), wout_ref[...])
    x1_ref[...] = x1
    tok = x1 * lax.rsqrt(jnp.mean(x1 * x1, axis=-1, keepdims=True) + RMS_EPS) * n2_ref[...]
    tok_b = tok.astype(BF16)
    tok_ref[...] = tok_b
    q_ref[...] = _dot(tok_b, wq_ref[...])


def _merge(x2, y_rw, y_ml, z_gate, gate_b, p_rw, p_ml, w_out, n2, w_q, tm=512):
    n = x2.shape[0]
    row = lambda w: pl.BlockSpec((tm, w), lambda i: (i, 0))
    full = lambda a: pl.BlockSpec(a.shape, lambda i: (0,) * a.ndim)
    qd = w_q.shape[1]
    return pl.pallas_call(
        _merge_kernel,
        grid=(n // tm,),
        in_specs=[row(D_MODEL), row(RW_WIDTH), row(ML_WIDTH), row(2 * D_MODEL), full(gate_b), full(p_rw),
                  full(p_ml), full(w_out), full(n2), full(w_q)],
        out_specs=[row(D_MODEL), row(D_MODEL), row(qd)],
        out_shape=[jax.ShapeDtypeStruct((n, D_MODEL), F32), jax.ShapeDtypeStruct((n, D_MODEL), BF16),
                   jax.ShapeDtypeStruct((n, qd), F32)],
        compiler_params=pltpu.CompilerParams(dimension_semantics=("parallel",), vmem_limit_bytes=VMEM_LIMIT),
        name="merge",
    )(x2, y_rw, y_ml, z_gate, gate_b, p_rw, p_ml, w_out, n2, w_q)


_CELLS = [(r0, r1) for r0 in range(PEER_TOPK) for r1 in range(PEER_TOPK) if (r0 + 1) * (r1 + 1) <= PEER_TOPK]
_CELL_ROWS = 56


def _top16_exact(s):
    kiota = _iota(s.shape, 0)
    rank = jnp.full(s.shape, float(PEER_KEYS - 1), F32)
    tops = []
    for r in range(PEER_TOPK):
        m = jnp.max(s, axis=0, keepdims=True)
        idx = jnp.min(jnp.where(s == m, kiota, PEER_KEYS), axis=0, keepdims=True)
        hit = kiota == idx
        rank = jnp.where(hit, float(r), rank)
        s = jnp.where(hit, NEG_INF, s)
        tops.append(m)
    return rank, jnp.concatenate(tops, axis=0)


def _top16_distinct(s):
    rank = jnp.full(s.shape, float(PEER_KEYS - 1), F32)
    tops = []
    for r in range(PEER_TOPK):
        m = jnp.max(s, axis=0, keepdims=True)
        hit = s == m
        rank = jnp.where(hit, float(r), rank)
        s = jnp.where(hit, NEG_INF, s)
        tops.append(m)
    return rank, jnp.concatenate(tops, axis=0)


_SORT16 = ((0, 1), (2, 3), (0, 2), (1, 3), (1, 2), (4, 5), (6, 7), (4, 6), (5, 7), (5, 6), (0, 4), (2, 6), (2, 4),
           (1, 5), (3, 7), (3, 5), (1, 2), (3, 4), (5, 6), (8, 9), (10, 11), (8, 10), (9, 11), (9, 10), (12, 13),
           (14, 15), (12, 14), (13, 15), (13, 14), (8, 12), (10, 14), (10, 12), (9, 13), (11, 15), (11, 13), (9, 10),
           (11, 12), (13, 14), (0, 8), (4, 12), (4, 8), (2, 10), (6, 14), (6, 10), (2, 4), (6, 8), (10, 12), (1, 9),
           (5, 13), (5, 9), (3, 11), (7, 15), (7, 11), (3, 5), (7, 9), (11, 13), (1, 2), (3, 4), (5, 6), (7, 8),
           (9, 10), (11, 12), (13, 14))


def _top16_values(s):
    n_tiles = PEER_KEYS // 8
    v = [s[8 * p:8 * (p + 1), :] for p in range(n_tiles)]
    for i, j in _SORT16:
        v[i], v[j] = jnp.maximum(v[i], v[j]), jnp.minimum(v[i], v[j])
    tops = []
    popped = jnp.zeros(v[0].shape, F32)
    for r in range(PEER_TOPK):
        m = jnp.max(v[0], axis=0, keepdims=True)
        hit = v[0] == m
        popped = popped + jnp.where(hit, 1.0, 0.0)
        tops.append(m)
        depth = PEER_TOPK - 1 - r
        for p in range(depth):
            v[p] = jnp.where(hit, v[p + 1], v[p])
    n_pop = jnp.sum(popped, axis=0, keepdims=True)
    for r in range(PEER_TOPK - 1):
        n_pop = n_pop + jnp.where(tops[r] == tops[r + 1], 1.0, 0.0)
    return jnp.concatenate(tops, axis=0), n_pop


def _select_kernel(q_ref, sk_ref, a_ref, l_ref, b_ref, c_ref, s_ref, rank_ref, tops_ref, ktops_ref):
    tn = q_ref.shape[0]
    groups = [(h, half) for h in range(PEER_HEADS) for half in range(2)]

    n_ranked = []
    for g, (h, half) in enumerate(groups):
        base = h * PEER_QDIM + half * PEER_HALF
        s = _dot_nt(sk_ref[h, half], q_ref[:, base:base + PEER_HALF])
        s_ref[g] = s
        if half == 0:
            tops, n_top = _top16_values(s)
            rank_ref[g] = s
            ktops_ref[h] = tops
        else:
            rank, tops = _top16_distinct(s)
            rank_ref[g] = rank
            n_top = jnp.sum((rank < float(PEER_TOPK)).astype(F32), axis=0, keepdims=True)
        tops_ref[g] = tops
        n_ranked.append(n_top)
    most_ranked = functools.reduce(jnp.maximum, n_ranked)

    @pl.when(jnp.max(most_ranked) > float(PEER_TOPK))
    def _():
        for g, (h, half) in enumerate(groups):
            @pl.when(jnp.max(n_ranked[g]) > float(PEER_TOPK))
            def _():
                rank_x, tops_x = _top16_exact(s_ref[g])
                tops_ref[g] = tops_x
                if half == 0:
                    rank_ref[g] = -rank_x
                    ktops_ref[h] = -_iota((PEER_TOPK, tn), 0).astype(F32)
                else:
                    rank_ref[g] = rank_x

    pos_col = jnp.concatenate(
        [jnp.full((1, 1), r0 * PEER_TOPK + r1, jnp.int32) for r0, r1 in _CELLS]
        + [jnp.full((_CELL_ROWS - len(_CELLS), 1), 1 << 20, jnp.int32)], axis=0)
    pos = jnp.broadcast_to(pos_col, (_CELL_ROWS, tn))
    for h in range(PEER_HEADS):
        tops0 = tops_ref[2 * h]
        tops1 = tops_ref[2 * h + 1]
        cand = jnp.concatenate([tops0[r0:r0 + 1] + tops1[r1:r1 + 1] for r0, r1 in _CELLS]
                               + [jnp.full((_CELL_ROWS - len(_CELLS), tn), NEG_INF, F32)], axis=0)
        work = cand
        sel = jnp.zeros(cand.shape, jnp.bool_)
        for _ in range(PEER_TOPK):
            m = jnp.max(work, axis=0, keepdims=True)
            idx = jnp.min(jnp.where(work == m, pos, 1 << 20), axis=0, keepdims=True)
            hit = pos == idx
            sel = jnp.logical_or(sel, hit)
            work = jnp.where(hit, NEG_INF, work)
        cmax = tops0[0:1] + tops1[0:1]
        e_sel = jnp.where(sel, jnp.exp(cand - cmax), 0.0)
        z = jnp.sum(e_sel, axis=0, keepdims=True)
        self_f = sel.astype(F32)
        counts = []
        start = 0
        for r0 in range(PEER_TOPK):
            n_cells = PEER_TOPK // (r0 + 1)
            counts.append(jnp.sum(self_f[start:start + n_cells, :], axis=0, keepdims=True))
            start += n_cells

        key0 = rank_ref[2 * h]
        ktops = ktops_ref[h]
        cnt = jnp.zeros(key0.shape, F32)
        for r0 in range(PEER_TOPK):
            cnt = jnp.where(key0 == ktops[r0:r0 + 1], counts[r0], cnt)
        a_ref[h] = 0.5 * jnp.exp(s_ref[2 * h] - tops0[0:1]) / z
        l_ref[h] = cnt
        b_ref[h] = jnp.exp(s_ref[2 * h + 1] - tops1[0:1]).astype(BF16)
        c_ref[h] = rank_ref[2 * h + 1].astype(BF16)


def _peer_select(q, sub_keys, tn=256):
    n = q.shape[0]
    shape = (PEER_HEADS, PEER_KEYS, n)
    ospec = pl.BlockSpec((PEER_HEADS, PEER_KEYS, tn), lambda i: (0, 0, i))
    return pl.pallas_call(
        _select_kernel,
        grid=(n // tn,),
        in_specs=[pl.BlockSpec((tn, q.shape[1]), lambda i: (i, 0)),
                  pl.BlockSpec(sub_keys.shape, lambda i: (0, 0, 0, 0))],
        out_specs=[ospec] * 4,
        out_shape=[jax.ShapeDtypeStruct(shape, F32), jax.ShapeDtypeStruct(shape, F32),
                   jax.ShapeDtypeStruct(shape, BF16), jax.ShapeDtypeStruct(shape, BF16)],
        scratch_shapes=[pltpu.VMEM((2 * PEER_HEADS, PEER_KEYS, tn), F32),
                        pltpu.VMEM((2 * PEER_HEADS, PEER_KEYS, tn), F32),
                        pltpu.VMEM((2 * PEER_HEADS, PEER_TOPK, tn), F32),
                        pltpu.VMEM((PEER_HEADS, PEER_TOPK, tn), F32)],
        compiler_params=pltpu.CompilerParams(dimension_semantics=("parallel",), vmem_limit_bytes=VMEM_LIMIT),
        name="peer_select",
    )(q, sub_keys)


GATE_COLS = 256
ROW_SLAB = 8


def _gate_block(a_slab, l_slab, r, il, cols, b_ref, c_ref, h_ref, m_ref):
    zero = jnp.zeros((PEER_KEYS, GATE_COLS), BF16)
    rows = slice(il * PEER_KEYS, (il + 1) * PEER_KEYS)
    gate = zero
    for h in range(PEER_HEADS):
        a_b = jnp.broadcast_to(a_slab[h, r:r + 1, cols], zero.shape).astype(BF16)
        l_b = jnp.broadcast_to(l_slab[h, r:r + 1, cols], zero.shape).astype(BF16)
        gate = gate + jnp.where(c_ref[h, :, cols] < l_b, b_ref[h, :, cols], zero) * a_b
    x = h_ref[rows, cols].astype(BF16)
    m_ref[rows, cols] = (x * (1.0 + lax.erf(x * (1.0 / math.sqrt(2.0))))) * gate


TILES_PER_STEP = 4


def _dense_kernel(tokt_ref, u_ref, vta_ref, vtb_ref, am_ref, a0_ref, ap_ref, lm_ref, l0_ref, lp_ref, b_ref, c_ref,
                  x1_ref, fg_ref, out_ref, acc_ref, h0_ref, h1_ref, m0_ref, m1_ref):
    e = pl.program_id(1)
    te, tn = h0_ref.shape
    n_tiles = PEER_EXPERTS // te
    n_full = n_tiles // TILES_PER_STEP
    rows_per_tile = te // PEER_KEYS
    assert 2 * rows_per_tile == ROW_SLAB and TILES_PER_STEP == 4
    h_bufs = (h0_ref, h1_ref)
    m_bufs = (m0_ref, m1_ref)
    slabs = ((am_ref, lm_ref), (a0_ref, l0_ref), (ap_ref, lp_ref))

    @pl.when(e == 0)
    def _():
        acc_ref[...] = jnp.zeros_like(acc_ref)
        m0_ref[...] = jnp.zeros_like(m0_ref)
        h1_ref[...] = jnp.zeros_like(h1_ref)

    def gate_tile(j, par, cols):
        s_idx, row0 = ((0, rows_per_tile), (1, 0), (1, rows_per_tile), (2, 0))[j]
        a_slab, l_slab = slabs[s_idx]
        for il in range(rows_per_tile):
            _gate_block(a_slab, l_slab, row0 + il, il, cols, b_ref, c_ref, h_bufs[1 - par], m_bufs[1 - par])

    def vt_tile(j):
        if j < 2:
            return vta_ref[:, j * te:(j + 1) * te]
        return vtb_ref[:, (j - 2) * te:(j - 1) * te]

    @pl.when(e < n_full)
    def _():
        for j in range(TILES_PER_STEP):
            par = j % 2
            for cg in range(tn // GATE_COLS):
                cols = slice(cg * GATE_COLS, (cg + 1) * GATE_COLS)
                acc_ref[:, cols] += _dot(vt_tile(j), m_bufs[par][:, cols])
                h_bufs[par][:, cols] = _dot(u_ref[j * te:(j + 1) * te, :], tokt_ref[:, cols])
                gate_tile(j, par, cols)

    @pl.when(e == n_full)
    def _():
        for cg in range(tn // GATE_COLS):
            cols = slice(cg * GATE_COLS, (cg + 1) * GATE_COLS)
            gate_tile(0, 0, cols)
            acc_ref[:, cols] += _dot(vt_tile(0), m0_ref[:, cols]) + _dot(vt_tile(1), m1_ref[:, cols])
        x2 = x1_ref[...] + acc_ref[...].T
        out_ref[...] = x2 * lax.rsqrt(jnp.mean(x2 * x2, axis=-1, keepdims=True) + RMS_EPS) * fg_ref[...]


def _peer_dense(tok_t, u_b, v_t, a, l, b, c, x1, final_g, tn=512, te=512):
    n = x1.shape[0]
    n_full = PEER_EXPERTS // (TILES_PER_STEP * te)
    n_pairs = PEER_EXPERTS // (2 * te)
    n_slabs = PEER_KEYS // ROW_SLAB
    sel_spec = pl.BlockSpec((PEER_HEADS, PEER_KEYS, tn), lambda t, e: (0, 0, t))
    slab_spec = lambda slab: pl.BlockSpec((PEER_HEADS, ROW_SLAB, tn), lambda t, e: (0, slab(e), t))
    return pl.pallas_call(
        _dense_kernel,
        grid=(n // tn, n_full + 1),
        in_specs=[pl.BlockSpec((D_MODEL, tn), lambda t, e: (0, t)),
                  pl.BlockSpec((TILES_PER_STEP * te, D_MODEL), lambda t, e: (jnp.minimum(e, n_full - 1), 0)),
                  pl.BlockSpec((D_MODEL, 2 * te), lambda t, e: (0, jnp.maximum(2 * e - 1, 0))),
                  pl.BlockSpec((D_MODEL, 2 * te), lambda t, e: (0, jnp.minimum(2 * e, n_pairs - 1))),
                  slab_spec(lambda e: jnp.maximum(2 * e - 1, 0)), slab_spec(lambda e: jnp.minimum(2 * e, n_slabs - 1)),
                  slab_spec(lambda e: jnp.minimum(2 * e + 1, n_slabs - 1)),
                  slab_spec(lambda e: jnp.maximum(2 * e - 1, 0)), slab_spec(lambda e: jnp.minimum(2 * e, n_slabs - 1)),
                  slab_spec(lambda e: jnp.minimum(2 * e + 1, n_slabs - 1)),
                  sel_spec, sel_spec,
                  pl.BlockSpec((tn, D_MODEL), lambda t, e: (t, 0)),
                  pl.BlockSpec((1, D_MODEL), lambda t, e: (0, 0))],
        out_specs=pl.BlockSpec((tn, D_MODEL), lambda t, e: (t, 0)),
        out_shape=jax.ShapeDtypeStruct((n, D_MODEL), F32),
        scratch_shapes=[pltpu.VMEM((D_MODEL, tn), F32), pltpu.VMEM((te, tn), F32), pltpu.VMEM((te, tn), F32),
                        pltpu.VMEM((te, tn), BF16), pltpu.VMEM((te, tn), BF16)],
        compiler_params=pltpu.CompilerParams(dimension_semantics=("parallel", "arbitrary"),
                                             vmem_limit_bytes=VMEM_LIMIT),
        name="peer_dense",
    )(tok_t, u_b, v_t, v_t, a, a, a, l, l, l, b, c, x1, final_g)


def _row(v):
    return v.reshape(1, -1).astype(F32)


def kernel(x, norm1_g, w_in, rw_mu, rw_w0, rw_w_up, rw_a0, rw_a_up, rw_g_up, rw_k_k, rw_k_a, rw_r_k, rw_gn_w, rw_gn_b, ml_conv_q_w, ml_conv_q_b, ml_conv_k_w, ml_conv_k_b, ml_b_i, ml_b_f, ml_norm_w, gate_b, p_rw, p_ml, w_out, norm2_g, peer_w_q, peer_sub_keys, peer_u, peer_v, final_g):
    bsz, t_len, d = x.shape
    n = bsz * t_len
    x2 = x.reshape(n, d)
    l = 0

    w = w_in[l]
    ml0 = RW_COLS
    if0 = RW_COLS + ML_MAIN
    gt0 = if0 + 2 * ML_HEADS
    wrw = w[:, :ml0].astype(BF16)
    wml = w[:, ml0:if0].astype(BF16)
    wif = jnp.pad(w[:, if0:gt0], ((0, 0), (0, LANES - 2 * ML_HEADS))).astype(BF16)
    wgt = w[:, gt0:].astype(BF16)
    z_rw, z_ml, z_if, z_gate = _inproj(x2, _row(norm1_g[l]), wrw, wml, wif, wgt)

    zero_lora = jnp.zeros((RW_DECAY_LORA, RW_WIDTH), F32)
    wup = jnp.concatenate([rw_w_up[l], zero_lora], axis=0).astype(BF16)
    aup = jnp.concatenate([zero_lora, rw_a_up[l]], axis=0).astype(BF16)
    y_rw = _rwkv(z_rw.reshape(bsz, t_len, RW_COLS), _row(rw_mu[l]), _row(rw_w0[l]), wup, _row(rw_a0[l]), aup,
                 rw_g_up[l].astype(BF16), _row(rw_k_k[l]), _row(rw_k_a[l]), _row(rw_r_k[l]), _row(rw_gn_w[l]),
                 _row(rw_gn_b[l]))

    conv_w = jnp.concatenate([ml_conv_q_w[l], ml_conv_k_w[l]], axis=1)
    conv_b = jnp.concatenate([ml_conv_q_b[l], ml_conv_k_b[l]]).reshape(1, -1)
    b_if = jnp.pad(jnp.concatenate([ml_b_i[l], ml_b_f[l]]), (0, LANES - 2 * ML_HEADS)).reshape(1, -1)
    y_ml = _mlstm(z_ml.reshape(bsz, t_len, ML_MAIN), z_if.reshape(bsz, t_len, LANES), conv_w, conv_b, b_if,
                  _row(ml_norm_w[l]))

    x1, tok, q = _merge(x2, y_rw.reshape(n, RW_WIDTH), y_ml.reshape(n, ML_WIDTH), z_gate, _row(gate_b[l]),
                        p_rw[l].astype(BF16), p_ml[l].astype(BF16), w_out[l].astype(BF16), _row(norm2_g[l]),
                        peer_w_q[l].astype(BF16))

    a, cnt, b, c = _peer_select(q, peer_sub_keys[l])
    out = _peer_dense(tok.T, peer_u[l].astype(BF16), peer_v[l].T.astype(BF16), a, cnt, b, c, x1, _row(final_g))
    return out.reshape(bsz, t_len, d)
```

```python
import functools
import math

import jax
import jax.numpy as jnp
from jax import lax
from jax.experimental import pallas as pl
from jax.experimental.pallas import tpu as pltpu

F32 = jnp.float32
BF16 = jnp.bfloat16

D_MODEL = 1024
CHUNK = 64
RMS_EPS = 1e-6
RW_HEADS = 8
RW_HEAD_DIM = 64
RW_WIDTH = 512
RW_DECAY_LORA = 64
RW_AAA_LORA = 64
RW_GATE_LORA = 128
RW_GN_EPS = 64e-5
RW_COLS = 3 * RW_WIDTH + RW_DECAY_LORA + RW_AAA_LORA + RW_GATE_LORA
RW_PAIRS = RW_HEADS // 2
ML_HEADS = 4
ML_HEAD_DIM = 128
ML_WIDTH = 512
ML_CONV = 4
ML_NORM_EPS = 1e-5
ML_MAIN = 4 * ML_WIDTH
PEER_HEADS = 8
PEER_KEYS = 128
PEER_EXPERTS = PEER_KEYS * PEER_KEYS
PEER_QDIM = 256
PEER_HALF = 128
PEER_TOPK = 16

LANES = 128
VMEM_LIMIT = 56 * 1024 * 1024

NEG_INF = float("-inf")


def _dot(a, b):
    return jnp.dot(a, b, preferred_element_type=F32)


def _dot_nt(a, b):
    return lax.dot_general(a, b, (((1,), (1,)), ((), ())), preferred_element_type=F32)


def _dot_tn(a, b):
    return lax.dot_general(a, b, (((0,), (0,)), ((), ())), preferred_element_type=F32)


def _sigmoid(x):
    return 1.0 / (1.0 + jnp.exp(-x))


def _softplus(x):
    return jnp.maximum(x, 0.0) + jnp.log(1.0 + jnp.exp(-jnp.abs(x)))


def _iota(shape, dim):
    return lax.broadcasted_iota(jnp.int32, shape, dim)


def _split2(x):
    hi = x.astype(BF16)
    return hi, (x - hi.astype(F32)).astype(BF16)


def _dot_exact_rhs(a, b_bf16):
    hi, lo = _split2(a)
    return _dot(hi, b_bf16) + _dot(lo, b_bf16)


def _dot_exact_lhs(a_bf16, b):
    hi = b.astype(BF16)
    rem = b - hi.astype(F32)
    mid = rem.astype(BF16)
    lo = (rem - mid.astype(F32)).astype(BF16)
    return _dot(a_bf16, hi) + _dot(a_bf16, mid) + _dot(a_bf16, lo)


def _inproj_kernel(x_ref, g_ref, wrw_ref, wml_ref, wif_ref, wgt_ref, zrw_ref, zml_ref, zif_ref, zgt_ref):
    x = x_ref[...]
    h = x * lax.rsqrt(jnp.mean(x * x, axis=-1, keepdims=True) + RMS_EPS) * g_ref[...]
    hb = h.astype(BF16)
    zrw_ref[...] = _dot(hb, wrw_ref[...])
    zml_ref[...] = _dot(hb, wml_ref[...])
    zif_ref[...] = _dot(hb, wif_ref[...])
    zgt_ref[...] = _dot(hb, wgt_ref[...])


def _inproj(x2, g, wrw, wml, wif, wgt, tm=512):
    n = x2.shape[0]
    row = lambda w: pl.BlockSpec((tm, w), lambda i: (i, 0))
    full = lambda a: pl.BlockSpec(a.shape, lambda i: (0,) * a.ndim)
    return pl.pallas_call(
        _inproj_kernel,
        grid=(n // tm,),
        in_specs=[row(D_MODEL), full(g), full(wrw), full(wml), full(wif), full(wgt)],
        out_specs=[row(wrw.shape[1]), row(wml.shape[1]), row(wif.shape[1]), row(wgt.shape[1])],
        out_shape=[jax.ShapeDtypeStruct((n, w.shape[1]), F32) for w in (wrw, wml, wif, wgt)],
        compiler_params=pltpu.CompilerParams(dimension_semantics=("parallel",), vmem_limit_bytes=VMEM_LIMIT),
        name="inproj",
    )(x2, g, wrw, wml, wif, wgt)


def _bd(xw):
    xb = xw.astype(BF16)
    left = _iota(xb.shape, 1) < RW_HEAD_DIM
    zero = jnp.zeros_like(xb)
    return jnp.concatenate([jnp.where(left, xb, zero), jnp.where(left, zero, xb)], axis=0)


def _mm(xw, yw):
    return _dot(xw.astype(BF16), _bd(yw))


def _rwkv_kernel(z_ref, mu_ref, w0_ref, wup_ref, a0_ref, aup_ref, gup_ref, kk_ref, ka_ref, rk_ref,
                 gnw_ref, gnb_ref, y_ref, prev_ref, st_ref):
    c = pl.program_id(1)
    nb = z_ref.shape[0]

    @pl.when(c == 0)
    def _():
        prev_ref[...] = jnp.zeros_like(prev_ref)
        st_ref[...] = jnp.zeros_like(st_ref)

    ii = _iota((CHUNK, LANES), 0)
    jj = _iota((CHUNK, LANES), 1) % RW_HEAD_DIM
    strict = jj < ii
    incl = jj <= ii
    eye_w = (jj == ii).astype(F32)
    same16 = (ii // 16) == (jj // 16)
    same32 = (ii // 32) == (jj // 32)
    tri = (_iota((CHUNK, CHUNK), 1) <= _iota((CHUNK, CHUNK), 0)).astype(BF16)
    r2 = _iota((LANES, LANES), 0)
    c2 = _iota((LANES, LANES), 1)
    blk = (r2 // RW_HEAD_DIM) == (c2 // RW_HEAD_DIM)
    ones_bd = blk.astype(BF16)
    eye2 = r2 == c2
    first = _iota((CHUNK, RW_COLS), 0) == 0
    o1, o2, o3 = RW_WIDTH, 2 * RW_WIDTH, 3 * RW_WIDTH

    items = []
    for b in range(nb):
        z = z_ref[b]
        z_prev = jnp.where(first, jnp.broadcast_to(prev_ref[b, 0:1, :], z.shape), pltpu.roll(z, 1, 0))
        prev_ref[b, 0:1, :] = z[CHUNK - 1:CHUNK, :]
        zs = z + (z_prev - z) * mu_ref[...]
        r = zs[:, :o1]
        k = zs[:, o1:o2]
        v = zs[:, o2:o3]
        zwa = zs[:, o3:o3 + LANES]
        zg = zs[:, o3 + LANES:]
        w_log = -_softplus(-(w0_ref[...] + _dot(jnp.tanh(zwa).astype(BF16), wup_ref[...]))) - 0.5
        logd = -jnp.exp(w_log)
        a = _sigmoid(a0_ref[...] + _dot(zwa.astype(BF16), aup_ref[...]))
        g = _dot(_sigmoid(zg).astype(BF16), gup_ref[...])
        kk_raw = k * kk_ref[...]
        k = k * (1.0 + (a - 1.0) * ka_ref[...])
        rkr = r * k * rk_ref[...]
        for p in range(RW_PAIRS):
            sl = slice(p * LANES, (p + 1) * LANES)
            items.append(dict(b=b, p=p, sl=sl, r=r[:, sl], k=k[:, sl], v=v[:, sl], a=a[:, sl], ld=logd[:, sl],
                              kk=kk_raw[:, sl], rkr=rkr[:, sl], g=g[:, sl]))

    def each(fn):
        return [fn(it) for it in items]

    def put(name, vals):
        for it, val in zip(items, vals):
            it[name] = val

    put("ss", each(lambda it: _dot_exact_rhs(it["kk"] * it["kk"], ones_bd)))
    put("lcum", each(lambda it: _dot_exact_lhs(tri, it["ld"])))
    put("bonus", each(lambda it: _dot_exact_rhs(it["rkr"], ones_bd)))
    for it in items:
        kk_n = it["kk"] / jnp.maximum(jnp.sqrt(it["ss"]), 1e-12)
        lcum = it["lcum"]
        l_end = lcum[CHUNK - 1:CHUNK, :]
        p_inv = jnp.exp(-lcum)
        to_end = jnp.exp(l_end - lcum)
        beta = kk_n * it["a"]
        it["l_end"] = l_end
        it["rt"] = it["r"] * jnp.exp(lcum)
        it["kt"] = kk_n * jnp.exp(lcum - it["ld"])
        it["kh_end"] = it["k"] * to_end
        it["bh_end"] = beta * to_end
        it["lhs"] = jnp.concatenate([it["kt"], it["rt"]], axis=0).astype(BF16)
        it["rhs"] = jnp.concatenate([_bd(it["k"] * p_inv), _bd(beta * p_inv)], axis=0)
    put("sc", each(lambda it: _dot_nt(it["lhs"], it["rhs"])))
    for it in items:
        sc = it["sc"]
        it["a_vk"] = jnp.where(strict, sc[:CHUNK, :LANES], 0.0)
        it["n_w"] = jnp.where(strict, sc[:CHUNK, LANES:], 0.0)
        it["a_rk"] = jnp.where(incl, sc[CHUNK:, :LANES], 0.0)
        it["a_rb"] = jnp.where(incl, sc[CHUNK:, LANES:], 0.0)
        it["d1"] = jnp.where(same16, it["n_w"], 0.0)

    put("av", each(lambda it: _mm(it["a_vk"], it["v"])))
    put("d2", each(lambda it: _mm(it["d1"], it["d1"])))
    put("d4", each(lambda it: _mm(it["d2"], it["d2"])))
    put("t", each(lambda it: _mm(eye_w - it["d1"], eye_w + it["d2"])))
    put("d8", each(lambda it: _mm(it["d4"], it["d4"])))
    put("t", each(lambda it: _mm(it["t"], eye_w + it["d4"])))
    put("t", each(lambda it: _mm(it["t"], eye_w + it["d8"])))
    off32_mask = jnp.logical_and(same32, jnp.logical_not(same16))
    put("tmp", each(lambda it: _mm(it["t"], jnp.where(off32_mask, it["n_w"], 0.0))))
    put("t", each(lambda it: it["t"] - _mm(it["tmp"], it["t"])))
    put("tmp", each(lambda it: _mm(it["t"], jnp.where(same32, 0.0, it["n_w"]))))
    put("t", each(lambda it: it["t"] - _mm(it["tmp"], it["t"])))

    put("x_w", each(lambda it: _mm(it["t"], it["av"])))
    put("w_w", each(lambda it: _mm(it["t"], it["kt"])))
    put("st", each(lambda it: st_ref[it["b"], it["p"]]))
    put("st_b", each(lambda it: it["st"].astype(BF16)))
    put("u", each(lambda it: -(_dot(it["w_w"].astype(BF16), it["st_b"]) + it["x_w"])))
    put("y", each(lambda it: _dot(it["rt"].astype(BF16), it["st_b"]) + _mm(it["a_rk"], it["v"])
                  + _mm(it["a_rb"], it["u"])))
    put("d_st", each(lambda it: _dot_tn(jnp.concatenate([it["kh_end"], it["bh_end"]], axis=0).astype(BF16),
                                        jnp.concatenate([it["v"], it["u"]], axis=0).astype(BF16))))
    for it in items:
        p_col = jnp.sum(jnp.where(eye2, jnp.broadcast_to(jnp.exp(it["l_end"]), (LANES, LANES)), 0.0),
                        axis=1, keepdims=True)
        st_ref[it["b"], it["p"]] = it["st"] * p_col + jnp.where(blk, it["d_st"], 0.0)

    put("mean", each(lambda it: _dot_exact_rhs(it["y"], ones_bd) * (1.0 / RW_HEAD_DIM)))
    put("yc", each(lambda it: it["y"] - it["mean"]))
    put("var", each(lambda it: _dot_exact_rhs(it["yc"] * it["yc"], ones_bd) * (1.0 / RW_HEAD_DIM)))
    for it in items:
        yn = it["yc"] * lax.rsqrt(it["var"] + RW_GN_EPS) * gnw_ref[:, it["sl"]] + gnb_ref[:, it["sl"]]
        y_ref[it["b"], :, it["sl"]] = (yn + it["bonus"] * it["v"]) * it["g"]


def _rwkv(z_rw, mu, w0, wup, a0, aup, gup, k_k, k_a, r_k, gn_w, gn_b, nb=4):
    bsz, t_len, _ = z_rw.shape
    full = lambda a: pl.BlockSpec(a.shape, lambda b, c: (0,) * a.ndim)
    params = (mu, w0, wup, a0, aup, gup, k_k, k_a, r_k, gn_w, gn_b)
    return pl.pallas_call(
        _rwkv_kernel,
        grid=(bsz // nb, t_len // CHUNK),
        in_specs=[pl.BlockSpec((nb, CHUNK, RW_COLS), lambda b, c: (b, c, 0))] + [full(a) for a in params],
        out_specs=pl.BlockSpec((nb, CHUNK, RW_WIDTH), lambda b, c: (b, c, 0)),
        out_shape=jax.ShapeDtypeStruct((bsz, t_len, RW_WIDTH), F32),
        scratch_shapes=[pltpu.VMEM((nb, 8, RW_COLS), F32), pltpu.VMEM((nb, RW_PAIRS, LANES, LANES), F32)],
        compiler_params=pltpu.CompilerParams(dimension_semantics=("parallel", "arbitrary"),
                                             vmem_limit_bytes=VMEM_LIMIT),
        name="rwkv7",
    )(z_rw, *params)


def _cummax_rows(x):
    rows = _iota(x.shape, 0)
    d = 1
    while d < x.shape[0]:
        x = jnp.maximum(x, jnp.where(rows < d, NEG_INF, pltpu.roll(x, d, 0)))
        d *= 2
    return x


def _dot_exact_rhs3(a, b_bf16):
    hi = a.astype(BF16)
    rem = a - hi.astype(F32)
    mid = rem.astype(BF16)
    lo = (rem - mid.astype(F32)).astype(BF16)
    return _dot(hi, b_bf16) + _dot(mid, b_bf16) + _dot(lo, b_bf16)


def _mlstm_kernel(z_ref, zif_ref, cw_ref, cb_ref, bif_ref, nw_ref, h_ref, conv_ref, c_ref, n_ref, m_ref):
    c = pl.program_id(1)
    nb = z_ref.shape[0]

    @pl.when(c == 0)
    def _():
        conv_ref[:, 0:8, :] = jnp.zeros((nb, 8, 2 * ML_WIDTH), F32)
        c_ref[...] = jnp.zeros_like(c_ref)
        n_ref[...] = jnp.zeros_like(n_ref)
        m_ref[...] = jnp.zeros_like(m_ref)

    tri = (_iota((CHUNK, CHUNK), 1) <= _iota((CHUNK, CHUNK), 0))
    tri_b = tri.astype(BF16)
    ones_sq = jnp.ones((LANES, LANES), BF16)
    ones_tall = jnp.ones((CHUNK, LANES), BF16)
    src = _iota((LANES, 2 * ML_HEADS * LANES), 0)
    blk_id = _iota((LANES, 2 * ML_HEADS * LANES), 1) // LANES
    head = blk_id // 2
    is_diff = (blk_id % 2) == 0
    sel = jnp.where(src == head + ML_HEADS, jnp.where(is_diff, -1.0, 1.0),
                    jnp.where(jnp.logical_and(is_diff, src == head), 1.0, 0.0)).astype(BF16)
    gate_lane = _iota((CHUNK, LANES), 1)

    items = []
    for b in range(nb):
        conv_ref[b, 8:8 + CHUNK, :] = z_ref[b, :, 0:2 * ML_WIDTH]
        acc = jnp.broadcast_to(cb_ref[...], (CHUNK, 2 * ML_WIDTH))
        for j in range(ML_CONV):
            off = 8 - (ML_CONV - 1) + j
            acc = acc + cw_ref[j:j + 1, :] * conv_ref[b, off:off + CHUNK, :]
        conv_ref[b, 0:8, :] = conv_ref[b, CHUNK:CHUNK + 8, :]
        qk = acc * _sigmoid(acc)
        q = qk[:, :ML_WIDTH]
        k = qk[:, ML_WIDTH:] * (ML_HEAD_DIM ** -0.5)
        v = z_ref[b, :, 2 * ML_WIDTH:3 * ML_WIDTH]
        o_pre = z_ref[b, :, 3 * ML_WIDTH:4 * ML_WIDTH]

        gz = zif_ref[b] + bif_ref[...]
        lf = -_softplus(-gz)
        bcum = _dot_exact_lhs(tri_b, lf)
        gates = jnp.where(gate_lane < ML_HEADS, gz, bcum)
        col_b = _dot_exact_rhs3(gates, sel)
        gates_t = gates.T
        for h in range(ML_HEADS):
            sl = slice(h * ML_HEAD_DIM, (h + 1) * ML_HEAD_DIM)
            items.append(dict(b=b, h=h, sl=sl, q=q[:, sl], k=k[:, sl], v=v[:, sl], o=o_pre[:, sl],
                              diff_col=col_b[:, 2 * h * LANES:(2 * h + 1) * LANES],
                              bc_col=col_b[:, (2 * h + 1) * LANES:(2 * h + 2) * LANES],
                              diff_row=gates_t[h:h + 1, :] - gates_t[ML_HEADS + h:ML_HEADS + h + 1, :]))

    def each(fn):
        return [fn(it) for it in items]

    def put(name, vals):
        for it, val in zip(items, vals):
            it[name] = val

    for it in items:
        it["qb"] = it["q"].astype(BF16)
        it["kb"] = it["k"].astype(BF16)
        it["c_in"] = c_ref[it["b"], it["h"]]
        it["n_in"] = n_ref[it["b"], it["h"], 0:1, :]
        it["m_in"] = m_ref[it["b"], it["h"], 0:1, :]
        it["g_end"] = it["bc_col"][CHUNK - 1:CHUNK, :]
        it["m_loc"] = jnp.max(it["diff_col"], axis=0, keepdims=True) + it["g_end"]
        wts = jnp.exp(it["diff_col"] + (it["g_end"] - it["m_loc"]))
        it["k_w"] = it["k"] * wts
        it["m_t"] = it["bc_col"] + jnp.maximum(_cummax_rows(it["diff_col"]), it["m_in"])
        d_log = jnp.where(tri, it["bc_col"][:, :CHUNK] + it["diff_row"], NEG_INF)
        it["p"] = jnp.exp(d_log - it["m_t"][:, :CHUNK])
        it["inter_w"] = jnp.exp(it["bc_col"] + it["m_in"] - it["m_t"])
        it["v_ext"] = jnp.concatenate([it["v"].astype(BF16), ones_tall], axis=1)
        n_rep = jnp.broadcast_to(it["n_in"], (ML_HEAD_DIM, ML_HEAD_DIM))
        it["c_ext"] = jnp.concatenate([it["c_in"], n_rep], axis=0).astype(BF16)
    put("qk", each(lambda it: _dot_nt(it["qb"], it["kb"])))
    put("qc", each(lambda it: _dot_nt(it["qb"], it["c_ext"])))
    put("c_loc", each(lambda it: _dot_tn(it["v_ext"][:, :ML_HEAD_DIM], it["k_w"].astype(BF16))))
    put("sv", each(lambda it: _dot((it["qk"] * it["p"]).astype(BF16), it["v_ext"])))
    for it in items:
        d = ML_HEAD_DIM
        num = it["sv"][:, :d] + it["inter_w"] * it["qc"][:, :d]
        den = it["sv"][:, d:] + it["inter_w"] * it["qc"][:, d:]
        it["hh"] = num / jnp.maximum(jnp.abs(den), jnp.exp(-it["m_t"]))
    put("mean", each(lambda it: _dot_exact_rhs(it["hh"], ones_sq) * (1.0 / ML_HEAD_DIM)))
    put("hc", each(lambda it: it["hh"] - it["mean"]))
    put("var", each(lambda it: _dot_exact_rhs(it["hc"] * it["hc"], ones_sq) * (1.0 / ML_HEAD_DIM)))
    for it in items:
        hn = it["hc"] * lax.rsqrt(it["var"] + ML_NORM_EPS) * nw_ref[:, it["sl"]]
        h_ref[it["b"], :, it["sl"]] = hn * _sigmoid(it["o"])

        m_new = jnp.maximum(it["g_end"] + it["m_in"], it["m_loc"])
        s_old = jnp.exp(it["g_end"] + it["m_in"] - m_new)
        s_new = jnp.exp(it["m_loc"] - m_new)
        n_loc = jnp.sum(it["k_w"], axis=0, keepdims=True)
        c_ref[it["b"], it["h"]] = s_old * it["c_in"] + s_new * it["c_loc"]
        n_ref[it["b"], it["h"]] = jnp.broadcast_to(s_old * it["n_in"] + s_new * n_loc, (8, ML_HEAD_DIM))
        m_ref[it["b"], it["h"]] = jnp.broadcast_to(m_new, (8, LANES))


def _mlstm(z_ml, z_if, conv_w, conv_b, b_if, norm_w, nb=4):
    bsz, t_len, _ = z_ml.shape
    full = lambda a: pl.BlockSpec(a.shape, lambda b, c: (0,) * a.ndim)
    params = (conv_w, conv_b, b_if, norm_w)
    return pl.pallas_call(
        _mlstm_kernel,
        grid=(bsz // nb, t_len // CHUNK),
        in_specs=[pl.BlockSpec((nb, CHUNK, ML_MAIN), lambda b, c: (b, c, 0)),
                  pl.BlockSpec((nb, CHUNK, LANES), lambda b, c: (b, c, 0))] + [full(a) for a in params],
        out_specs=pl.BlockSpec((nb, CHUNK, ML_WIDTH), lambda b, c: (b, c, 0)),
        out_shape=jax.ShapeDtypeStruct((bsz, t_len, ML_WIDTH), F32),
        scratch_shapes=[pltpu.VMEM((nb, 8 + CHUNK, 2 * ML_WIDTH), F32),
                        pltpu.VMEM((nb, ML_HEADS, ML_HEAD_DIM, ML_HEAD_DIM), F32),
                        pltpu.VMEM((nb, ML_HEADS, 8, ML_HEAD_DIM), F32),
                        pltpu.VMEM((nb, ML_HEADS, 8, LANES), F32)],
        compiler_params=pltpu.CompilerParams(dimension_semantics=("parallel", "arbitrary"),
                                             vmem_limit_bytes=VMEM_LIMIT),
        name="mlstm",
    )(z_ml, z_if, *params)


def _merge_kernel(x_ref, yrw_ref, yml_ref, zg_ref, gb_ref, prw_ref, pml_ref, wout_ref, n2_ref, wq_ref,
                  x1_ref, tok_ref, q_ref):
    gate = _sigmoid(zg_ref[...] + gb_ref[...])
    m_rw = _dot(yrw_ref[...].astype(BF16), prw_ref[...])
    m_ml = _dot(yml_ref[...].astype(BF16), pml_ref[...])
    merged = gate[:, :D_MODEL] * m_rw + gate[:, D_MODEL:] * m_ml
    x1 = x_ref[...] + _dot(merged.astype(BF16), wout_ref[...])
    x1_ref[...] = x1
    tok = x1 * lax.rsqrt(jnp.mean(x1 * x1, axis=-1, keepdims=True) + RMS_EPS) * n2_ref[...]
    tok_b = tok.astype(BF16)
    tok_ref[...] = tok_b
    q_ref[...] = _dot(tok_b, wq_ref[...])


def _merge(x2, y_rw, y_ml, z_gate, gate_b, p_rw, p_ml, w_out, n2, w_q, tm=512):
    n = x2.shape[0]
    row = lambda w: pl.BlockSpec((tm, w), lambda i: (i, 0))
    full = lambda a: pl.BlockSpec(a.shape, lambda i: (0,) * a.ndim)
    qd = w_q.shape[1]
    return pl.pallas_call(
        _merge_kernel,
        grid=(n // tm,),
        in_specs=[row(D_MODEL), row(RW_WIDTH), row(ML_WIDTH), row(2 * D_MODEL), full(gate_b), full(p_rw),
                  full(p_ml), full(w_out), full(n2), full(w_q)],
        out_specs=[row(D_MODEL), row(D_MODEL), row(qd)],
        out_shape=[jax.ShapeDtypeStruct((n, D_MODEL), F32), jax.ShapeDtypeStruct((n, D_MODEL), BF16),
                   jax.ShapeDtypeStruct((n, qd), F32)],
        compiler_params=pltpu.CompilerParams(dimension_semantics=("parallel",), vmem_limit_bytes=VMEM_LIMIT),
        name="merge",
    )(x2, y_rw, y_ml, z_gate, gate_b, p_rw, p_ml, w_out, n2, w_q)


_CELLS = [(r0, r1) for r0 in range(PEER_TOPK) for r1 in range(PEER_TOPK) if (r0 + 1) * (r1 + 1) <= PEER_TOPK]
_CELL_ROWS = 56


def _top16_exact(s):
    kiota = _iota(s.shape, 0)
    rank = jnp.full(s.shape, float(PEER_KEYS - 1), F32)
    tops = []
    for r in range(PEER_TOPK):
        m = jnp.max(s, axis=0, keepdims=True)
        idx = jnp.min(jnp.where(s == m, kiota, PEER_KEYS), axis=0, keepdims=True)
        hit = kiota == idx
        rank = jnp.where(hit, float(r), rank)
        s = jnp.where(hit, NEG_INF, s)
        tops.append(m)
    return rank, jnp.concatenate(tops, axis=0)


def _top16_distinct(s):
    rank = jnp.full(s.shape, float(PEER_KEYS - 1), F32)
    tops = []
    for r in range(PEER_TOPK):
        m = jnp.max(s, axis=0, keepdims=True)
        hit = s == m
        rank = jnp.where(hit, float(r), rank)
        s = jnp.where(hit, NEG_INF, s)
        tops.append(m)
    return rank, jnp.concatenate(tops, axis=0)


_SORT16 = ((0, 1), (2, 3), (0, 2), (1, 3), (1, 2), (4, 5), (6, 7), (4, 6), (5, 7), (5, 6), (0, 4), (2, 6), (2, 4),
           (1, 5), (3, 7), (3, 5), (1, 2), (3, 4), (5, 6), (8, 9), (10, 11), (8, 10), (9, 11), (9, 10), (12, 13),
           (14, 15), (12, 14), (13, 15), (13, 14), (8, 12), (10, 14), (10, 12), (9, 13), (11, 15), (11, 13), (9, 10),
           (11, 12), (13, 14), (0, 8), (4, 12), (4, 8), (2, 10), (6, 14), (6, 10), (2, 4), (6, 8), (10, 12), (1, 9),
           (5, 13), (5, 9), (3, 11), (7, 15), (7, 11), (3, 5), (7, 9), (11, 13), (1, 2), (3, 4), (5, 6), (7, 8),
           (9, 10), (11, 12), (13, 14))


def _top16_values(s):
    n_tiles = PEER_KEYS // 8
    v = [s[8 * p:8 * (p + 1), :] for p in range(n_tiles)]
    for i, j in _SORT16:
        v[i], v[j] = jnp.maximum(v[i], v[j]), jnp.minimum(v[i], v[j])
    tops = []
    popped = jnp.zeros(v[0].shape, F32)
    for r in range(PEER_TOPK):
        m = jnp.max(v[0], axis=0, keepdims=True)
        hit = v[0] == m
        popped = popped + jnp.where(hit, 1.0, 0.0)
        tops.append(m)
        depth = PEER_TOPK - 1 - r
        for p in range(depth):
            v[p] = jnp.where(hit, v[p + 1], v[p])
    n_pop = jnp.sum(popped, axis=0, keepdims=True)
    for r in range(PEER_TOPK - 1):
        n_pop = n_pop + jnp.where(tops[r] == tops[r + 1], 1.0, 0.0)
    return jnp.concatenate(tops, axis=0), n_pop


def _select_kernel(q_ref, sk_ref, a_ref, l_ref, b_ref, c_ref, s_ref, rank_ref, tops_ref, ktops_ref):
    tn = q_ref.shape[0]
    groups = [(h, half) for h in range(PEER_HEADS) for half in range(2)]

    n_ranked = []
    for g, (h, half) in enumerate(groups):
        base = h * PEER_QDIM + half * PEER_HALF
        s = _dot_nt(sk_ref[h, half], q_ref[:, base:base + PEER_HALF])
        s_ref[g] = s
        if half == 0:
            tops, n_top = _top16_values(s)
            rank_ref[g] = s
            ktops_ref[h] = tops
        else:
            rank, tops = _top16_distinct(s)
            rank_ref[g] = rank
            n_top = jnp.sum((rank < float(PEER_TOPK)).astype(F32), axis=0, keepdims=True)
        tops_ref[g] = tops
        n_ranked.append(n_top)
    most_ranked = functools.reduce(jnp.maximum, n_ranked)

    @pl.when(jnp.max(most_ranked) > float(PEER_TOPK))
    def _():
        for g, (h, half) in enumerate(groups):
            @pl.when(jnp.max(n_ranked[g]) > float(PEER_TOPK))
            def _():
                rank_x, tops_x = _top16_exact(s_ref[g])
                tops_ref[g] = tops_x
                if half == 0:
                    rank_ref[g] = -rank_x
                    ktops_ref[h] = -_iota((PEER_TOPK, tn), 0).astype(F32)
                else:
                    rank_ref[g] = rank_x

    pos_col = jnp.concatenate(
        [jnp.full((1, 1), r0 * PEER_TOPK + r1, jnp.int32) for r0, r1 in _CELLS]
        + [jnp.full((_CELL_ROWS - len(_CELLS), 1), 1 << 20, jnp.int32)], axis=0)
    pos = jnp.broadcast_to(pos_col, (_CELL_ROWS, tn))
    for h in range(PEER_HEADS):
        tops0 = tops_ref[2 * h]
        tops1 = tops_ref[2 * h + 1]
        cand = jnp.concatenate([tops0[r0:r0 + 1] + tops1[r1:r1 + 1] for r0, r1 in _CELLS]
                               + [jnp.full((_CELL_ROWS - len(_CELLS), tn), NEG_INF, F32)], axis=0)
        work = cand
        sel = jnp.zeros(cand.shape, jnp.bool_)
        for _ in range(PEER_TOPK):
            m = jnp.max(work, axis=0, keepdims=True)
            idx = jnp.min(jnp.where(work == m, pos, 1 << 20), axis=0, keepdims=True)
            hit = pos == idx
            sel = jnp.logical_or(sel, hit)
            work = jnp.where(hit, NEG_INF, work)
        cmax = tops0[0:1] + tops1[0:1]
        e_sel = jnp.where(sel, jnp.exp(cand - cmax), 0.0)
        z = jnp.sum(e_sel, axis=0, keepdims=True)
        self_f = sel.astype(F32)
        counts = []
        start = 0
        for r0 in range(PEER_TOPK):
            n_cells = PEER_TOPK // (r0 + 1)
            counts.append(jnp.sum(self_f[start:start + n_cells, :], axis=0, keepdims=True))
            start += n_cells

        key0 = rank_ref[2 * h]
        ktops = ktops_ref[h]
        cnt = jnp.zeros(key0.shape, F32)
        for r0 in range(PEER_TOPK):
            cnt = jnp.where(key0 == ktops[r0:r0 + 1], counts[r0], cnt)
        a_ref[h] = 0.5 * jnp.exp(s_ref[2 * h] - tops0[0:1]) / z
        l_ref[h] = cnt
        b_ref[h] = jnp.exp(s_ref[2 * h + 1] - tops1[0:1]).astype(BF16)
        c_ref[h] = rank_ref[2 * h + 1].astype(BF16)


def _peer_select(q, sub_keys, tn=256):
    n = q.shape[0]
    shape = (PEER_HEADS, PEER_KEYS, n)
    ospec = pl.BlockSpec((PEER_HEADS, PEER_KEYS, tn), lambda i: (0, 0, i))
    return pl.pallas_call(
        _select_kernel,
        grid=(n // tn,),
        in_specs=[pl.BlockSpec((tn, q.shape[1]), lambda i: (i, 0)),
                  pl.BlockSpec(sub_keys.shape, lambda i: (0, 0, 0, 0))],
        out_specs=[ospec] * 4,
        out_shape=[jax.ShapeDtypeStruct(shape, F32), jax.ShapeDtypeStruct(shape, F32),
                   jax.ShapeDtypeStruct(shape, BF16), jax.ShapeDtypeStruct(shape, BF16)],
        scratch_shapes=[pltpu.VMEM((2 * PEER_HEADS, PEER_KEYS, tn), F32),
                        pltpu.VMEM((2 * PEER_HEADS, PEER_KEYS, tn), F32),
                        pltpu.VMEM((2 * PEER_HEADS, PEER_TOPK, tn), F32),
                        pltpu.VMEM((PEER_HEADS, PEER_TOPK, tn), F32)],
        compiler_params=pltpu.CompilerParams(dimension_semantics=("parallel",), vmem_limit_bytes=VMEM_LIMIT),
        name="peer_select",
    )(q, sub_keys)


GATE_COLS = 256
ROW_SLAB = 8


def _gate_block(a_slab, l_slab, r, il, cols, b_ref, c_ref, h_ref, m_ref):
    zero = jnp.zeros((PEER_KEYS, GATE_COLS), BF16)
    rows = slice(il * PEER_KEYS, (il + 1) * PEER_KEYS)
    gate = zero
    for h in range(PEER_HEADS):
        a_b = jnp.broadcast_to(a_slab[h, r:r + 1, cols], zero.shape).astype(BF16)
        l_b = jnp.broadcast_to(l_slab[h, r:r + 1, cols], zero.shape).astype(BF16)
        gate = gate + jnp.where(c_ref[h, :, cols] < l_b, b_ref[h, :, cols], zero) * a_b
    x = h_ref[rows, cols].astype(BF16)
    m_ref[rows, cols] = (x * (1.0 + lax.erf(x * (1.0 / math.sqrt(2.0))))) * gate


TILES_PER_STEP = 4


def _dense_kernel(tokt_ref, u_ref, vta_ref, vtb_ref, am_ref, a0_ref, ap_ref, lm_ref, l0_ref, lp_ref, b_ref, c_ref,
                  x1_ref, fg_ref, out_ref, acc_ref, h0_ref, h1_ref, m0_ref, m1_ref):
    e = pl.program_id(1)
    te, tn = h0_ref.shape
    n_tiles = PEER_EXPERTS // te
    n_full = n_tiles // TILES_PER_STEP
    rows_per_tile = te // PEER_KEYS
    assert 2 * rows_per_tile == ROW_SLAB and TILES_PER_STEP == 4
    h_bufs = (h0_ref, h1_ref)
    m_bufs = (m0_ref, m1_ref)
    slabs = ((am_ref, lm_ref), (a0_ref, l0_ref), (ap_ref, lp_ref))

    @pl.when(e == 0)
    def _():
        acc_ref[...] = jnp.zeros_like(acc_ref)
        m0_ref[...] = jnp.zeros_like(m0_ref)
        h1_ref[...] = jnp.zeros_like(h1_ref)

    def gate_tile(j, par, cols):
        s_idx, row0 = ((0, rows_per_tile), (1, 0), (1, rows_per_tile), (2, 0))[j]
        a_slab, l_slab = slabs[s_idx]
        for il in range(rows_per_tile):
            _gate_block(a_slab, l_slab, row0 + il, il, cols, b_ref, c_ref, h_bufs[1 - par], m_bufs[1 - par])

    def vt_tile(j):
        if j < 2:
            return vta_ref[:, j * te:(j + 1) * te]
        return vtb_ref[:, (j - 2) * te:(j - 1) * te]

    @pl.when(e < n_full)
    def _():
        for j in range(TILES_PER_STEP):
            par = j % 2
            for cg in range(tn // GATE_COLS):
                cols = slice(cg * GATE_COLS, (cg + 1) * GATE_COLS)
                acc_ref[:, cols] += _dot(vt_tile(j), m_bufs[par][:, cols])
                h_bufs[par][:, cols] = _dot(u_ref[j * te:(j + 1) * te, :], tokt_ref[:, cols])
                gate_tile(j, par, cols)

    @pl.when(e == n_full)
    def _():
        for cg in range(tn // GATE_COLS):
            cols = slice(cg * GATE_COLS, (cg + 1) * GATE_COLS)
            gate_tile(0, 0, cols)
            acc_ref[:, cols] += _dot(vt_tile(0), m0_ref[:, cols]) + _dot(vt_tile(1), m1_ref[:, cols])
        x2 = x1_ref[...] + acc_ref[...].T
        out_ref[...] = x2 * lax.rsqrt(jnp.mean(x2 * x2, axis=-1, keepdims=True) + RMS_EPS) * fg_ref[...]


def _peer_dense(tok_t, u_b, v_t, a, l, b, c, x1, final_g, tn=512, te=512):
    n = x1.shape[0]
    n_full = PEER_EXPERTS // (TILES_PER_STEP * te)
    n_pairs = PEER_EXPERTS // (2 * te)
    n_slabs = PEER_KEYS // ROW_SLAB
    sel_spec = pl.BlockSpec((PEER_HEADS, PEER_KEYS, tn), lambda t, e: (0, 0, t))
    slab_spec = lambda slab: pl.BlockSpec((PEER_HEADS, ROW_SLAB, tn), lambda t, e: (0, slab(e), t))
    return pl.pallas_call(
        _dense_kernel,
        grid=(n // tn, n_full + 1),
        in_specs=[pl.BlockSpec((D_MODEL, tn), lambda t, e: (0, t)),
                  pl.BlockSpec((TILES_PER_STEP * te, D_MODEL), lambda t, e: (jnp.minimum(e, n_full - 1), 0)),
                  pl.BlockSpec((D_MODEL, 2 * te), lambda t, e: (0, jnp.maximum(2 * e - 1, 0))),
                  pl.BlockSpec((D_MODEL, 2 * te), lambda t, e: (0, jnp.minimum(2 * e, n_pairs - 1))),
                  slab_spec(lambda e: jnp.maximum(2 * e - 1, 0)), slab_spec(lambda e: jnp.minimum(2 * e, n_slabs - 1)),
                  slab_spec(lambda e: jnp.minimum(2 * e + 1, n_slabs - 1)),
                  slab_spec(lambda e: jnp.maximum(2 * e - 1, 0)), slab_spec(lambda e: jnp.minimum(2 * e, n_slabs - 1)),
                  slab_spec(lambda e: jnp.minimum(2 * e + 1, n_slabs - 1)),
                  sel_spec, sel_spec,
                  pl.BlockSpec((tn, D_MODEL), lambda t, e: (t, 0)),
                  pl.BlockSpec((1, D_MODEL), lambda t, e: (0, 0))],
        out_specs=pl.BlockSpec((tn, D_MODEL), lambda t, e: (t, 0)),
        out_shape=jax.ShapeDtypeStruct((n, D_MODEL), F32),
        scratch_shapes=[pltpu.VMEM((D_MODEL, tn), F32), pltpu.VMEM((te, tn), F32), pltpu.VMEM((te, tn), F32),
                        pltpu.VMEM((te, tn), BF16), pltpu.VMEM((te, tn), BF16)],
        compiler_params=pltpu.CompilerParams(dimension_semantics=("parallel", "arbitrary"),
                                             vmem_limit_bytes=VMEM_LIMIT),
        name="peer_dense",
    )(tok_t, u_b, v_t, v_t, a, a, a, l, l, l, b, c, x1, final_g)


def _row(v):
    return v.reshape(1, -1).astype(F32)


def kernel(x, norm1_g, w_in, rw_mu, rw_w0, rw_w_up, rw_a0, rw_a_up, rw_g_up, rw_k_k, rw_k_a, rw_r_k, rw_gn_w, rw_gn_b, ml_conv_q_w, ml_conv_q_b, ml_conv_k_w, ml_conv_k_b, ml_b_i, ml_b_f, ml_norm_w, gate_b, p_rw, p_ml, w_out, norm2_g, peer_w_q, peer_sub_keys, peer_u, peer_v, final_g):
    bsz, t_len, d = x.shape
    n = bsz * t_len
    x2 = x.reshape(n, d)
    l = 0

    w = w_in[l]
    ml0 = RW_COLS
    if0 = RW_COLS + ML_MAIN
    gt0 = if0 + 2 * ML_HEADS
    wrw = w[:, :ml0].astype(BF16)
    wml = w[:, ml0:if0].astype(BF16)
    wif = jnp.pad(w[:, if0:gt0], ((0, 0), (0, LANES - 2 * ML_HEADS))).astype(BF16)
    wgt = w[:, gt0:].astype(BF16)
    z_rw, z_ml, z_if, z_gate = _inproj(x2, _row(norm1_g[l]), wrw, wml, wif, wgt)

    zero_lora = jnp.zeros((RW_DECAY_LORA, RW_WIDTH), F32)
    wup = jnp.concatenate([rw_w_up[l], zero_lora], axis=0).astype(BF16)
    aup = jnp.concatenate([zero_lora, rw_a_up[l]], axis=0).astype(BF16)
    y_rw = _rwkv(z_rw.reshape(bsz, t_len, RW_COLS), _row(rw_mu[l]), _row(rw_w0[l]), wup, _row(rw_a0[l]), aup,
                 rw_g_up[l].astype(BF16), _row(rw_k_k[l]), _row(rw_k_a[l]), _row(rw_r_k[l]), _row(rw_gn_w[l]),
                 _row(rw_gn_b[l]))

    conv_w = jnp.concatenate([ml_conv_q_w[l], ml_conv_k_w[l]], axis=1)
    conv_b = jnp.concatenate([ml_conv_q_b[l], ml_conv_k_b[l]]).reshape(1, -1)
    b_if = jnp.pad(jnp.concatenate([ml_b_i[l], ml_b_f[l]]), (0, LANES - 2 * ML_HEADS)).reshape(1, -1)
    y_ml = _mlstm(z_ml.reshape(bsz, t_len, ML_MAIN), z_if.reshape(bsz, t_len, LANES), conv_w, conv_b, b_if,
                  _row(ml_norm_w[l]))

    x1, tok, q = _merge(x2, y_rw.reshape(n, RW_WIDTH), y_ml.reshape(n, ML_WIDTH), z_gate, _row(gate_b[l]),
                        p_rw[l].astype(BF16), p_ml[l].astype(BF16), w_out[l].astype(BF16), _row(norm2_g[l]),
                        peer_w_q[l].astype(BF16))

    a, cnt, b, c = _peer_select(q, peer_sub_keys[l])
    out = _peer_dense(tok.T, peer_u[l].astype(BF16), peer_v[l].astype(BF16).T, a, cnt, b, c, x1, _row(final_g))
    return out.reshape(bsz, t_len, d)
```

```python
import functools
import math

import jax
import jax.numpy as jnp
from jax import lax
from jax.experimental import pallas as pl
from jax.experimental.pallas import tpu as pltpu

F32 = jnp.float32
BF16 = jnp.bfloat16

D_MODEL = 1024
CHUNK = 64
RMS_EPS = 1e-6
RW_HEADS = 8
RW_HEAD_DIM = 64
RW_WIDTH = 512
RW_DECAY_LORA = 64
RW_AAA_LORA = 64
RW_GATE_LORA = 128
RW_GN_EPS = 64e-5
RW_COLS = 3 * RW_WIDTH + RW_DECAY_LORA + RW_AAA_LORA + RW_GATE_LORA
RW_PAIRS = RW_HEADS // 2
ML_HEADS = 4
ML_HEAD_DIM = 128
ML_WIDTH = 512
ML_CONV = 4
ML_NORM_EPS = 1e-5
ML_MAIN = 4 * ML_WIDTH
PEER_HEADS = 8
PEER_KEYS = 128
PEER_EXPERTS = PEER_KEYS * PEER_KEYS
PEER_QDIM = 256
PEER_HALF = 128
PEER_TOPK = 16

LANES = 128
VMEM_LIMIT = 56 * 1024 * 1024

NEG_INF = float("-inf")


def _dot(a, b):
    return jnp.dot(a, b, preferred_element_type=F32)


def _dot_nt(a, b):
    return lax.dot_general(a, b, (((1,), (1,)), ((), ())), preferred_element_type=F32)


def _dot_tn(a, b):
    return lax.dot_general(a, b, (((0,), (0,)), ((), ())), preferred_element_type=F32)


def _sigmoid(x):
    return 1.0 / (1.0 + jnp.exp(-x))


def _softplus(x):
    return jnp.maximum(x, 0.0) + jnp.log(1.0 + jnp.exp(-jnp.abs(x)))


def _iota(shape, dim):
    return lax.broadcasted_iota(jnp.int32, shape, dim)


def _split2(x):
    hi = x.astype(BF16)
    return hi, (x - hi.astype(F32)).astype(BF16)


def _dot_exact_rhs(a, b_bf16):
    hi, lo = _split2(a)
    return _dot(hi, b_bf16) + _dot(lo, b_bf16)


def _dot_exact_lhs(a_bf16, b):
    hi = b.astype(BF16)
    rem = b - hi.astype(F32)
    mid = rem.astype(BF16)
    lo = (rem - mid.astype(F32)).astype(BF16)
    return _dot(a_bf16, hi) + _dot(a_bf16, mid) + _dot(a_bf16, lo)


def _inproj_kernel(x_ref, g_ref, wrw_ref, wml_ref, wif_ref, wgt_ref, zrw_ref, zml_ref, zif_ref, zgt_ref):
    x = x_ref[...]
    h = x * lax.rsqrt(jnp.mean(x * x, axis=-1, keepdims=True) + RMS_EPS) * g_ref[...]
    hb = h.astype(BF16)
    zrw_ref[...] = _dot(hb, wrw_ref[...])
    zml_ref[...] = _dot(hb, wml_ref[...])
    zif_ref[...] = _dot(hb, wif_ref[...])
    zgt_ref[...] = _dot(hb, wgt_ref[...])


def _inproj(x2, g, wrw, wml, wif, wgt, tm=512):
    n = x2.shape[0]
    row = lambda w: pl.BlockSpec((tm, w), lambda i: (i, 0))
    full = lambda a: pl.BlockSpec(a.shape, lambda i: (0,) * a.ndim)
    return pl.pallas_call(
        _inproj_kernel,
        grid=(n // tm,),
        in_specs=[row(D_MODEL), full(g), full(wrw), full(wml), full(wif), full(wgt)],
        out_specs=[row(wrw.shape[1]), row(wml.shape[1]), row(wif.shape[1]), row(wgt.shape[1])],
        out_shape=[jax.ShapeDtypeStruct((n, w.shape[1]), F32) for w in (wrw, wml, wif, wgt)],
        compiler_params=pltpu.CompilerParams(dimension_semantics=("parallel",), vmem_limit_bytes=VMEM_LIMIT),
        name="inproj",
    )(x2, g, wrw, wml, wif, wgt)


def _bd(xw):
    xb = xw.astype(BF16)
    left = _iota(xb.shape, 1) < RW_HEAD_DIM
    zero = jnp.zeros_like(xb)
    return jnp.concatenate([jnp.where(left, xb, zero), jnp.where(left, zero, xb)], axis=0)


def _mm(xw, yw):
    return _dot(xw.astype(BF16), _bd(yw))


def _rwkv_kernel(z_ref, mu_ref, w0_ref, wup_ref, a0_ref, aup_ref, gup_ref, kk_ref, ka_ref, rk_ref,
                 gnw_ref, gnb_ref, y_ref, prev_ref, st_ref):
    c = pl.program_id(1)
    nb = z_ref.shape[0]

    @pl.when(c == 0)
    def _():
        prev_ref[...] = jnp.zeros_like(prev_ref)
        st_ref[...] = jnp.zeros_like(st_ref)

    ii = _iota((CHUNK, LANES), 0)
    jj = _iota((CHUNK, LANES), 1) % RW_HEAD_DIM
    strict = jj < ii
    incl = jj <= ii
    eye_w = (jj == ii).astype(F32)
    same16 = (ii // 16) == (jj // 16)
    same32 = (ii // 32) == (jj // 32)
    tri = (_iota((CHUNK, CHUNK), 1) <= _iota((CHUNK, CHUNK), 0)).astype(BF16)
    r2 = _iota((LANES, LANES), 0)
    c2 = _iota((LANES, LANES), 1)
    blk = (r2 // RW_HEAD_DIM) == (c2 // RW_HEAD_DIM)
    ones_bd = blk.astype(BF16)
    eye2 = r2 == c2
    first = _iota((CHUNK, RW_COLS), 0) == 0
    o1, o2, o3 = RW_WIDTH, 2 * RW_WIDTH, 3 * RW_WIDTH

    items = []
    for b in range(nb):
        z = z_ref[b]
        z_prev = jnp.where(first, jnp.broadcast_to(prev_ref[b, 0:1, :], z.shape), pltpu.roll(z, 1, 0))
        prev_ref[b, 0:1, :] = z[CHUNK - 1:CHUNK, :]
        zs = z + (z_prev - z) * mu_ref[...]
        r = zs[:, :o1]
        k = zs[:, o1:o2]
        v = zs[:, o2:o3]
        zwa = zs[:, o3:o3 + LANES]
        zg = zs[:, o3 + LANES:]
        w_log = -_softplus(-(w0_ref[...] + _dot(jnp.tanh(zwa).astype(BF16), wup_ref[...]))) - 0.5
        logd = -jnp.exp(w_log)
        a = _sigmoid(a0_ref[...] + _dot(zwa.astype(BF16), aup_ref[...]))
        g = _dot(_sigmoid(zg).astype(BF16), gup_ref[...])
        kk_raw = k * kk_ref[...]
        k = k * (1.0 + (a - 1.0) * ka_ref[...])
        rkr = r * k * rk_ref[...]
        for p in range(RW_PAIRS):
            sl = slice(p * LANES, (p + 1) * LANES)
            items.append(dict(b=b, p=p, sl=sl, r=r[:, sl], k=k[:, sl], v=v[:, sl], a=a[:, sl], ld=logd[:, sl],
                              kk=kk_raw[:, sl], rkr=rkr[:, sl], g=g[:, sl]))

    def each(fn):
        return [fn(it) for it in items]

    def put(name, vals):
        for it, val in zip(items, vals):
            it[name] = val

    put("ss", each(lambda it: _dot_exact_rhs(it["kk"] * it["kk"], ones_bd)))
    put("lcum", each(lambda it: _dot_exact_lhs(tri, it["ld"])))
    put("bonus", each(lambda it: _dot_exact_rhs(it["rkr"], ones_bd)))
    for it in items:
        kk_n = it["kk"] / jnp.maximum(jnp.sqrt(it["ss"]), 1e-12)
        lcum = it["lcum"]
        l_end = lcum[CHUNK - 1:CHUNK, :]
        p_inv = jnp.exp(-lcum)
        to_end = jnp.exp(l_end - lcum)
        beta = kk_n * it["a"]
        it["l_end"] = l_end
        it["rt"] = it["r"] * jnp.exp(lcum)
        it["kt"] = kk_n * jnp.exp(lcum - it["ld"])
        it["kh_end"] = it["k"] * to_end
        it["bh_end"] = beta * to_end
        it["lhs"] = jnp.concatenate([it["kt"], it["rt"]], axis=0).astype(BF16)
        it["rhs"] = jnp.concatenate([_bd(it["k"] * p_inv), _bd(beta * p_inv)], axis=0)
    put("sc", each(lambda it: _dot_nt(it["lhs"], it["rhs"])))
    for it in items:
        sc = it["sc"]
        it["a_vk"] = jnp.where(strict, sc[:CHUNK, :LANES], 0.0)
        it["n_w"] = jnp.where(strict, sc[:CHUNK, LANES:], 0.0)
        it["a_rk"] = jnp.where(incl, sc[CHUNK:, :LANES], 0.0)
        it["a_rb"] = jnp.where(incl, sc[CHUNK:, LANES:], 0.0)
        it["d1"] = jnp.where(same16, it["n_w"], 0.0)

    put("av", each(lambda it: _mm(it["a_vk"], it["v"])))
    put("d2", each(lambda it: _mm(it["d1"], it["d1"])))
    put("d4", each(lambda it: _mm(it["d2"], it["d2"])))
    put("t", each(lambda it: _mm(eye_w - it["d1"], eye_w + it["d2"])))
    put("d8", each(lambda it: _mm(it["d4"], it["d4"])))
    put("t", each(lambda it: _mm(it["t"], eye_w + it["d4"])))
    put("t", each(lambda it: _mm(it["t"], eye_w + it["d8"])))
    off32_mask = jnp.logical_and(same32, jnp.logical_not(same16))
    put("tmp", each(lambda it: _mm(it["t"], jnp.where(off32_mask, it["n_w"], 0.0))))
    put("t", each(lambda it: it["t"] - _mm(it["tmp"], it["t"])))
    put("tmp", each(lambda it: _mm(it["t"], jnp.where(same32, 0.0, it["n_w"]))))
    put("t", each(lambda it: it["t"] - _mm(it["tmp"], it["t"])))

    put("x_w", each(lambda it: _mm(it["t"], it["av"])))
    put("w_w", each(lambda it: _mm(it["t"], it["kt"])))
    put("st", each(lambda it: st_ref[it["b"], it["p"]]))
    put("st_b", each(lambda it: it["st"].astype(BF16)))
    put("u", each(lambda it: -(_dot(it["w_w"].astype(BF16), it["st_b"]) + it["x_w"])))
    put("y", each(lambda it: _dot(it["rt"].astype(BF16), it["st_b"]) + _mm(it["a_rk"], it["v"])
                  + _mm(it["a_rb"], it["u"])))
    put("d_st", each(lambda it: _dot_tn(jnp.concatenate([it["kh_end"], it["bh_end"]], axis=0).astype(BF16),
                                        jnp.concatenate([it["v"], it["u"]], axis=0).astype(BF16))))
    for it in items:
        p_col = jnp.sum(jnp.where(eye2, jnp.broadcast_to(jnp.exp(it["l_end"]), (LANES, LANES)), 0.0),
                        axis=1, keepdims=True)
        st_ref[it["b"], it["p"]] = it["st"] * p_col + jnp.where(blk, it["d_st"], 0.0)

    put("mean", each(lambda it: _dot_exact_rhs(it["y"], ones_bd) * (1.0 / RW_HEAD_DIM)))
    put("yc", each(lambda it: it["y"] - it["mean"]))
    put("var", each(lambda it: _dot_exact_rhs(it["yc"] * it["yc"], ones_bd) * (1.0 / RW_HEAD_DIM)))
    for it in items:
        yn = it["yc"] * lax.rsqrt(it["var"] + RW_GN_EPS) * gnw_ref[:, it["sl"]] + gnb_ref[:, it["sl"]]
        y_ref[it["b"], :, it["sl"]] = (yn + it["bonus"] * it["v"]) * it["g"]


def _rwkv(z_rw, mu, w0, wup, a0, aup, gup, k_k, k_a, r_k, gn_w, gn_b, nb=4):
    bsz, t_len, _ = z_rw.shape
    full = lambda a: pl.BlockSpec(a.shape, lambda b, c: (0,) * a.ndim)
    params = (mu, w0, wup, a0, aup, gup, k_k, k_a, r_k, gn_w, gn_b)
    return pl.pallas_call(
        _rwkv_kernel,
        grid=(bsz // nb, t_len // CHUNK),
        in_specs=[pl.BlockSpec((nb, CHUNK, RW_COLS), lambda b, c: (b, c, 0))] + [full(a) for a in params],
        out_specs=pl.BlockSpec((nb, CHUNK, RW_WIDTH), lambda b, c: (b, c, 0)),
        out_shape=jax.ShapeDtypeStruct((bsz, t_len, RW_WIDTH), F32),
        scratch_shapes=[pltpu.VMEM((nb, 8, RW_COLS), F32), pltpu.VMEM((nb, RW_PAIRS, LANES, LANES), F32)],
        compiler_params=pltpu.CompilerParams(dimension_semantics=("parallel", "arbitrary"),
                                             vmem_limit_bytes=VMEM_LIMIT),
        name="rwkv7",
    )(z_rw, *params)


def _cummax_rows(x):
    rows = _iota(x.shape, 0)
    d = 1
    while d < x.shape[0]:
        x = jnp.maximum(x, jnp.where(rows < d, NEG_INF, pltpu.roll(x, d, 0)))
        d *= 2
    return x


def _dot_exact_rhs3(a, b_bf16):
    hi = a.astype(BF16)
    rem = a - hi.astype(F32)
    mid = rem.astype(BF16)
    lo = (rem - mid.astype(F32)).astype(BF16)
    return _dot(hi, b_bf16) + _dot(mid, b_bf16) + _dot(lo, b_bf16)


def _mlstm_kernel(z_ref, zif_ref, cw_ref, cb_ref, bif_ref, nw_ref, h_ref, conv_ref, c_ref, n_ref, m_ref):
    c = pl.program_id(1)
    nb = z_ref.shape[0]

    @pl.when(c == 0)
    def _():
        conv_ref[:, 0:8, :] = jnp.zeros((nb, 8, 2 * ML_WIDTH), F32)
        c_ref[...] = jnp.zeros_like(c_ref)
        n_ref[...] = jnp.zeros_like(n_ref)
        m_ref[...] = jnp.zeros_like(m_ref)

    tri = (_iota((CHUNK, CHUNK), 1) <= _iota((CHUNK, CHUNK), 0))
    tri_b = tri.astype(BF16)
    ones_sq = jnp.ones((LANES, LANES), BF16)
    ones_tall = jnp.ones((CHUNK, LANES), BF16)
    src = _iota((LANES, 2 * ML_HEADS * LANES), 0)
    blk_id = _iota((LANES, 2 * ML_HEADS * LANES), 1) // LANES
    head = blk_id // 2
    is_diff = (blk_id % 2) == 0
    sel = jnp.where(src == head + ML_HEADS, jnp.where(is_diff, -1.0, 1.0),
                    jnp.where(jnp.logical_and(is_diff, src == head), 1.0, 0.0)).astype(BF16)
    gate_lane = _iota((CHUNK, LANES), 1)

    items = []
    for b in range(nb):
        xc = z_ref[b, :, 0:2 * ML_WIDTH]
        prev = conv_ref[b, 0:8, :]
        rows8 = _iota((8, 2 * ML_WIDTH), 0)
        acc = cb_ref[...] + cw_ref[ML_CONV - 1:ML_CONV, :] * xc
        for d in range(1, ML_CONV):
            rolled = pltpu.roll(xc, d, 0)
            head = jnp.where(rows8 < d, pltpu.roll(prev, d, 0), rolled[0:8])
            acc = acc + cw_ref[ML_CONV - 1 - d:ML_CONV - d, :] * jnp.concatenate([head, rolled[8:]], axis=0)
        conv_ref[b, 0:8, :] = xc[CHUNK - 8:CHUNK]
        qk = acc * _sigmoid(acc)
        q = qk[:, :ML_WIDTH]
        k = qk[:, ML_WIDTH:] * (ML_HEAD_DIM ** -0.5)
        v = z_ref[b, :, 2 * ML_WIDTH:3 * ML_WIDTH]
        o_pre = z_ref[b, :, 3 * ML_WIDTH:4 * ML_WIDTH]

        gz = zif_ref[b] + bif_ref[...]
        lf = -_softplus(-gz)
        bcum = _dot_exact_lhs(tri_b, lf)
        gates = jnp.where(gate_lane < ML_HEADS, gz, bcum)
        col_b = _dot_exact_rhs3(gates, sel)
        gates_t = gates.T
        for h in range(ML_HEADS):
            sl = slice(h * ML_HEAD_DIM, (h + 1) * ML_HEAD_DIM)
            items.append(dict(b=b, h=h, sl=sl, q=q[:, sl], k=k[:, sl], v=v[:, sl], o=o_pre[:, sl],
                              diff_col=col_b[:, 2 * h * LANES:(2 * h + 1) * LANES],
                              bc_col=col_b[:, (2 * h + 1) * LANES:(2 * h + 2) * LANES],
                              diff_row=gates_t[h:h + 1, :] - gates_t[ML_HEADS + h:ML_HEADS + h + 1, :]))

    def each(fn):
        return [fn(it) for it in items]

    def put(name, vals):
        for it, val in zip(items, vals):
            it[name] = val

    for it in items:
        it["qb"] = it["q"].astype(BF16)
        it["kb"] = it["k"].astype(BF16)
        it["c_in"] = c_ref[it["b"], it["h"]]
        it["n_in"] = n_ref[it["b"], it["h"], 0:1, :]
        it["m_in"] = m_ref[it["b"], it["h"], 0:1, :]
        it["g_end"] = it["bc_col"][CHUNK - 1:CHUNK, :]
        it["m_loc"] = jnp.max(it["diff_col"], axis=0, keepdims=True) + it["g_end"]
        wts = jnp.exp(it["diff_col"] + (it["g_end"] - it["m_loc"]))
        it["k_w"] = it["k"] * wts
        it["m_t"] = it["bc_col"] + jnp.maximum(_cummax_rows(it["diff_col"]), it["m_in"])
        d_log = jnp.where(tri, it["bc_col"][:, :CHUNK] + it["diff_row"], NEG_INF)
        it["p"] = jnp.exp(d_log - it["m_t"][:, :CHUNK])
        it["inter_w"] = jnp.exp(it["bc_col"] + it["m_in"] - it["m_t"])
        it["v_ext"] = jnp.concatenate([it["v"].astype(BF16), ones_tall], axis=1)
        n_rep = jnp.broadcast_to(it["n_in"], (ML_HEAD_DIM, ML_HEAD_DIM))
        it["c_ext"] = jnp.concatenate([it["c_in"], n_rep], axis=0).astype(BF16)
    put("qk", each(lambda it: _dot_nt(it["qb"], it["kb"])))
    put("qc", each(lambda it: _dot_nt(it["qb"], it["c_ext"])))
    put("c_loc", each(lambda it: _dot_tn(it["v_ext"][:, :ML_HEAD_DIM], it["k_w"].astype(BF16))))
    put("sv", each(lambda it: _dot((it["qk"] * it["p"]).astype(BF16), it["v_ext"])))
    for it in items:
        d = ML_HEAD_DIM
        num = it["sv"][:, :d] + it["inter_w"] * it["qc"][:, :d]
        den = it["sv"][:, d:] + it["inter_w"] * it["qc"][:, d:]
        it["hh"] = num / jnp.maximum(jnp.abs(den), jnp.exp(-it["m_t"]))
    put("mean", each(lambda it: _dot_exact_rhs(it["hh"], ones_sq) * (1.0 / ML_HEAD_DIM)))
    put("hc", each(lambda it: it["hh"] - it["mean"]))
    put("var", each(lambda it: _dot_exact_rhs(it["hc"] * it["hc"], ones_sq) * (1.0 / ML_HEAD_DIM)))
    for it in items:
        hn = it["hc"] * lax.rsqrt(it["var"] + ML_NORM_EPS) * nw_ref[:, it["sl"]]
        h_ref[it["b"], :, it["sl"]] = hn * _sigmoid(it["o"])

        m_new = jnp.maximum(it["g_end"] + it["m_in"], it["m_loc"])
        s_old = jnp.exp(it["g_end"] + it["m_in"] - m_new)
        s_new = jnp.exp(it["m_loc"] - m_new)
        n_loc = jnp.sum(it["k_w"], axis=0, keepdims=True)
        c_ref[it["b"], it["h"]] = s_old * it["c_in"] + s_new * it["c_loc"]
        n_ref[it["b"], it["h"]] = jnp.broadcast_to(s_old * it["n_in"] + s_new * n_loc, (8, ML_HEAD_DIM))
        m_ref[it["b"], it["h"]] = jnp.broadcast_to(m_new, (8, LANES))


def _mlstm(z_ml, z_if, conv_w, conv_b, b_if, norm_w, nb=4):
    bsz, t_len, _ = z_ml.shape
    full = lambda a: pl.BlockSpec(a.shape, lambda b, c: (0,) * a.ndim)
    params = (conv_w, conv_b, b_if, norm_w)
    return pl.pallas_call(
        _mlstm_kernel,
        grid=(bsz // nb, t_len // CHUNK),
        in_specs=[pl.BlockSpec((nb, CHUNK, ML_MAIN), lambda b, c: (b, c, 0)),
                  pl.BlockSpec((nb, CHUNK, LANES), lambda b, c: (b, c, 0))] + [full(a) for a in params],
        out_specs=pl.BlockSpec((nb, CHUNK, ML_WIDTH), lambda b, c: (b, c, 0)),
        out_shape=jax.ShapeDtypeStruct((bsz, t_len, ML_WIDTH), F32),
        scratch_shapes=[pltpu.VMEM((nb, 8 + CHUNK, 2 * ML_WIDTH), F32),
                        pltpu.VMEM((nb, ML_HEADS, ML_HEAD_DIM, ML_HEAD_DIM), F32),
                        pltpu.VMEM((nb, ML_HEADS, 8, ML_HEAD_DIM), F32),
                        pltpu.VMEM((nb, ML_HEADS, 8, LANES), F32)],
        compiler_params=pltpu.CompilerParams(dimension_semantics=("parallel", "arbitrary"),
                                             vmem_limit_bytes=VMEM_LIMIT),
        name="mlstm",
    )(z_ml, z_if, *params)


def _merge_kernel(x_ref, yrw_ref, yml_ref, zg_ref, gb_ref, prw_ref, pml_ref, wout_ref, n2_ref, wq_ref,
                  x1_ref, tok_ref, q_ref):
    gate = _sigmoid(zg_ref[...] + gb_ref[...])
    m_rw = _dot(yrw_ref[...].astype(BF16), prw_ref[...])
    m_ml = _dot(yml_ref[...].astype(BF16), pml_ref[...])
    merged = gate[:, :D_MODEL] * m_rw + gate[:, D_MODEL:] * m_ml
    x1 = x_ref[...] + _dot(merged.astype(BF16), wout_ref[...])
    x1_ref[...] = x1
    tok = x1 * lax.rsqrt(jnp.mean(x1 * x1, axis=-1, keepdims=True) + RMS_EPS) * n2_ref[...]
    tok_b = tok.astype(BF16)
    tok_ref[...] = tok_b
    q_ref[...] = _dot(tok_b, wq_ref[...])


def _merge(x2, y_rw, y_ml, z_gate, gate_b, p_rw, p_ml, w_out, n2, w_q, tm=512):
    n = x2.shape[0]
    row = lambda w: pl.BlockSpec((tm, w), lambda i: (i, 0))
    full = lambda a: pl.BlockSpec(a.shape, lambda i: (0,) * a.ndim)
    qd = w_q.shape[1]
    return pl.pallas_call(
        _merge_kernel,
        grid=(n // tm,),
        in_specs=[row(D_MODEL), row(RW_WIDTH), row(ML_WIDTH), row(2 * D_MODEL), full(gate_b), full(p_rw),
                  full(p_ml), full(w_out), full(n2), full(w_q)],
        out_specs=[row(D_MODEL), row(D_MODEL), row(qd)],
        out_shape=[jax.ShapeDtypeStruct((n, D_MODEL), F32), jax.ShapeDtypeStruct((n, D_MODEL), BF16),
                   jax.ShapeDtypeStruct((n, qd), F32)],
        compiler_params=pltpu.CompilerParams(dimension_semantics=("parallel",), vmem_limit_bytes=VMEM_LIMIT),
        name="merge",
    )(x2, y_rw, y_ml, z_gate, gate_b, p_rw, p_ml, w_out, n2, w_q)


_CELLS = [(r0, r1) for r0 in range(PEER_TOPK) for r1 in range(PEER_TOPK) if (r0 + 1) * (r1 + 1) <= PEER_TOPK]
_CELL_ROWS = 56


def _top16_exact(s):
    kiota = _iota(s.shape, 0)
    rank = jnp.full(s.shape, float(PEER_KEYS - 1), F32)
    tops = []
    for r in range(PEER_TOPK):
        m = jnp.max(s, axis=0, keepdims=True)
        idx = jnp.min(jnp.where(s == m, kiota, PEER_KEYS), axis=0, keepdims=True)
        hit = kiota == idx
        rank = jnp.where(hit, float(r), rank)
        s = jnp.where(hit, NEG_INF, s)
        tops.append(m)
    return rank, jnp.concatenate(tops, axis=0)


def _top16_distinct(s):
    rank = jnp.full(s.shape, float(PEER_KEYS - 1), F32)
    tops = []
    for r in range(PEER_TOPK):
        m = jnp.max(s, axis=0, keepdims=True)
        hit = s == m
        rank = jnp.where(hit, float(r), rank)
        s = jnp.where(hit, NEG_INF, s)
        tops.append(m)
    return rank, jnp.concatenate(tops, axis=0)


_SORT16 = ((0, 1), (2, 3), (0, 2), (1, 3), (1, 2), (4, 5), (6, 7), (4, 6), (5, 7), (5, 6), (0, 4), (2, 6), (2, 4),
           (1, 5), (3, 7), (3, 5), (1, 2), (3, 4), (5, 6), (8, 9), (10, 11), (8, 10), (9, 11), (9, 10), (12, 13),
           (14, 15), (12, 14), (13, 15), (13, 14), (8, 12), (10, 14), (10, 12), (9, 13), (11, 15), (11, 13), (9, 10),
           (11, 12), (13, 14), (0, 8), (4, 12), (4, 8), (2, 10), (6, 14), (6, 10), (2, 4), (6, 8), (10, 12), (1, 9),
           (5, 13), (5, 9), (3, 11), (7, 15), (7, 11), (3, 5), (7, 9), (11, 13), (1, 2), (3, 4), (5, 6), (7, 8),
           (9, 10), (11, 12), (13, 14))


def _top16_values(s):
    n_tiles = PEER_KEYS // 8
    v = [s[8 * p:8 * (p + 1), :] for p in range(n_tiles)]
    for i, j in _SORT16:
        v[i], v[j] = jnp.maximum(v[i], v[j]), jnp.minimum(v[i], v[j])
    tops = []
    popped = jnp.zeros(v[0].shape, F32)
    for r in range(PEER_TOPK):
        m = jnp.max(v[0], axis=0, keepdims=True)
        hit = v[0] == m
        popped = popped + jnp.where(hit, 1.0, 0.0)
        tops.append(m)
        depth = PEER_TOPK - 1 - r
        for p in range(depth):
            v[p] = jnp.where(hit, v[p + 1], v[p])
    n_pop = jnp.sum(popped, axis=0, keepdims=True)
    for r in range(PEER_TOPK - 1):
        n_pop = n_pop + jnp.where(tops[r] == tops[r + 1], 1.0, 0.0)
    return jnp.concatenate(tops, axis=0), n_pop


def _select_kernel(q_ref, sk_ref, a_ref, l_ref, b_ref, c_ref, s_ref, rank_ref, tops_ref, ktops_ref):
    tn = q_ref.shape[0]
    groups = [(h, half) for h in range(PEER_HEADS) for half in range(2)]

    n_ranked = []
    for g, (h, half) in enumerate(groups):
        base = h * PEER_QDIM + half * PEER_HALF
        s = _dot_nt(sk_ref[h, half], q_ref[:, base:base + PEER_HALF])
        s_ref[g] = s
        if half == 0:
            tops, n_top = _top16_values(s)
            rank_ref[g] = s
            ktops_ref[h] = tops
        else:
            tops, n_top = _top16_values(s)
            rank = jnp.zeros(s.shape, F32)
            for r in range(PEER_TOPK):
                rank = rank + jnp.where(s < tops[r:r + 1], 1.0, 0.0)
            rank_ref[g] = rank
        tops_ref[g] = tops
        n_ranked.append(n_top)
    most_ranked = functools.reduce(jnp.maximum, n_ranked)

    @pl.when(jnp.max(most_ranked) > float(PEER_TOPK))
    def _():
        for g, (h, half) in enumerate(groups):
            @pl.when(jnp.max(n_ranked[g]) > float(PEER_TOPK))
            def _():
                rank_x, tops_x = _top16_exact(s_ref[g])
                tops_ref[g] = tops_x
                if half == 0:
                    rank_ref[g] = -rank_x
                    ktops_ref[h] = -_iota((PEER_TOPK, tn), 0).astype(F32)
                else:
                    rank_ref[g] = rank_x

    pos_col = jnp.concatenate(
        [jnp.full((1, 1), r0 * PEER_TOPK + r1, jnp.int32) for r0, r1 in _CELLS]
        + [jnp.full((_CELL_ROWS - len(_CELLS), 1), 1 << 20, jnp.int32)], axis=0)
    pos = jnp.broadcast_to(pos_col, (_CELL_ROWS, tn))
    for h in range(PEER_HEADS):
        tops0 = tops_ref[2 * h]
        tops1 = tops_ref[2 * h + 1]
        cand = jnp.concatenate([tops0[r0:r0 + 1] + tops1[r1:r1 + 1] for r0, r1 in _CELLS]
                               + [jnp.full((_CELL_ROWS - len(_CELLS), tn), NEG_INF, F32)], axis=0)
        work = cand
        sel = jnp.zeros(cand.shape, jnp.bool_)
        for _ in range(PEER_TOPK):
            m = jnp.max(work, axis=0, keepdims=True)
            idx = jnp.min(jnp.where(work == m, pos, 1 << 20), axis=0, keepdims=True)
            hit = pos == idx
            sel = jnp.logical_or(sel, hit)
            work = jnp.where(hit, NEG_INF, work)
        cmax = tops0[0:1] + tops1[0:1]
        e_sel = jnp.where(sel, jnp.exp(cand - cmax), 0.0)
        z = jnp.sum(e_sel, axis=0, keepdims=True)
        self_f = sel.astype(F32)
        counts = []
        start = 0
        for r0 in range(PEER_TOPK):
            n_cells = PEER_TOPK // (r0 + 1)
            counts.append(jnp.sum(self_f[start:start + n_cells, :], axis=0, keepdims=True))
            start += n_cells

        key0 = rank_ref[2 * h]
        ktops = ktops_ref[h]
        cnt = jnp.zeros(key0.shape, F32)
        for r0 in range(PEER_TOPK):
            cnt = jnp.where(key0 == ktops[r0:r0 + 1], counts[r0], cnt)
        a_ref[h] = 0.5 * jnp.exp(s_ref[2 * h] - tops0[0:1]) / z
        l_ref[h] = cnt
        b_ref[h] = jnp.exp(s_ref[2 * h + 1] - tops1[0:1]).astype(BF16)
        c_ref[h] = rank_ref[2 * h + 1].astype(BF16)


def _peer_select(q, sub_keys, tn=256):
    n = q.shape[0]
    shape = (PEER_HEADS, PEER_KEYS, n)
    ospec = pl.BlockSpec((PEER_HEADS, PEER_KEYS, tn), lambda i: (0, 0, i))
    return pl.pallas_call(
        _select_kernel,
        grid=(n // tn,),
        in_specs=[pl.BlockSpec((tn, q.shape[1]), lambda i: (i, 0)),
                  pl.BlockSpec(sub_keys.shape, lambda i: (0, 0, 0, 0))],
        out_specs=[ospec] * 4,
        out_shape=[jax.ShapeDtypeStruct(shape, F32), jax.ShapeDtypeStruct(shape, F32),
                   jax.ShapeDtypeStruct(shape, BF16), jax.ShapeDtypeStruct(shape, BF16)],
        scratch_shapes=[pltpu.VMEM((2 * PEER_HEADS, PEER_KEYS, tn), F32),
                        pltpu.VMEM((2 * PEER_HEADS, PEER_KEYS, tn), F32),
                        pltpu.VMEM((2 * PEER_HEADS, PEER_TOPK, tn), F32),
                        pltpu.VMEM((PEER_HEADS, PEER_TOPK, tn), F32)],
        compiler_params=pltpu.CompilerParams(dimension_semantics=("parallel",), vmem_limit_bytes=VMEM_LIMIT),
        name="peer_select",
    )(q, sub_keys)


GATE_COLS = 256
ROW_SLAB = 8


def _gate_block(a_slab, l_slab, r, il, cols, b_ref, c_ref, h_ref, m_ref):
    zero = jnp.zeros((PEER_KEYS, GATE_COLS), BF16)
    rows = slice(il * PEER_KEYS, (il + 1) * PEER_KEYS)
    gate = zero
    for h in range(PEER_HEADS):
        a_b = jnp.broadcast_to(a_slab[h, r:r + 1, cols], zero.shape).astype(BF16)
        l_b = jnp.broadcast_to(l_slab[h, r:r + 1, cols], zero.shape).astype(BF16)
        gate = gate + jnp.where(c_ref[h, :, cols] < l_b, b_ref[h, :, cols], zero) * a_b
    x = h_ref[rows, cols].astype(BF16)
    m_ref[rows, cols] = (x * (1.0 + lax.erf(x * (1.0 / math.sqrt(2.0))))) * gate


TILES_PER_STEP = 4


def _dense_kernel(tokt_ref, u_ref, vta_ref, vtb_ref, am_ref, a0_ref, ap_ref, lm_ref, l0_ref, lp_ref, b_ref, c_ref,
                  x1_ref, fg_ref, out_ref, acc_ref, h0_ref, h1_ref, m0_ref, m1_ref):
    e = pl.program_id(1)
    te, tn = h0_ref.shape
    n_tiles = PEER_EXPERTS // te
    n_full = n_tiles // TILES_PER_STEP
    rows_per_tile = te // PEER_KEYS
    assert 2 * rows_per_tile == ROW_SLAB and TILES_PER_STEP == 4
    h_bufs = (h0_ref, h1_ref)
    m_bufs = (m0_ref, m1_ref)
    slabs = ((am_ref, lm_ref), (a0_ref, l0_ref), (ap_ref, lp_ref))

    @pl.when(e == 0)
    def _():
        acc_ref[...] = jnp.zeros_like(acc_ref)
        m0_ref[...] = jnp.zeros_like(m0_ref)
        h1_ref[...] = jnp.zeros_like(h1_ref)

    def gate_tile(j, par, cols):
        s_idx, row0 = ((0, rows_per_tile), (1, 0), (1, rows_per_tile), (2, 0))[j]
        a_slab, l_slab = slabs[s_idx]
        for il in range(rows_per_tile):
            _gate_block(a_slab, l_slab, row0 + il, il, cols, b_ref, c_ref, h_bufs[1 - par], m_bufs[1 - par])

    def vt_tile(j):
        if j < 2:
            return vta_ref[:, j * te:(j + 1) * te]
        return vtb_ref[:, (j - 2) * te:(j - 1) * te]

    @pl.when(e < n_full)
    def _():
        for j in range(TILES_PER_STEP):
            par = j % 2
            for cg in range(tn // GATE_COLS):
                cols = slice(cg * GATE_COLS, (cg + 1) * GATE_COLS)
                acc_ref[:, cols] += _dot(vt_tile(j), m_bufs[par][:, cols])
                h_bufs[par][:, cols] = _dot(u_ref[j * te:(j + 1) * te, :], tokt_ref[:, cols])
                gate_tile(j, par, cols)

    @pl.when(e == n_full)
    def _():
        for cg in range(tn // GATE_COLS):
            cols = slice(cg * GATE_COLS, (cg + 1) * GATE_COLS)
            gate_tile(0, 0, cols)
            acc_ref[:, cols] += _dot(vt_tile(0), m0_ref[:, cols]) + _dot(vt_tile(1), m1_ref[:, cols])
        x2 = x1_ref[...] + acc_ref[...].T
        out_ref[...] = x2 * lax.rsqrt(jnp.mean(x2 * x2, axis=-1, keepdims=True) + RMS_EPS) * fg_ref[...]


def _peer_dense(tok_t, u_b, v_t, a, l, b, c, x1, final_g, tn=512, te=512):
    n = x1.shape[0]
    n_full = PEER_EXPERTS // (TILES_PER_STEP * te)
    n_pairs = PEER_EXPERTS // (2 * te)
    n_slabs = PEER_KEYS // ROW_SLAB
    sel_spec = pl.BlockSpec((PEER_HEADS, PEER_KEYS, tn), lambda t, e: (0, 0, t))
    slab_spec = lambda slab: pl.BlockSpec((PEER_HEADS, ROW_SLAB, tn), lambda t, e: (0, slab(e), t))
    return pl.pallas_call(
        _dense_kernel,
        grid=(n // tn, n_full + 1),
        in_specs=[pl.BlockSpec((D_MODEL, tn), lambda t, e: (0, t)),
                  pl.BlockSpec((TILES_PER_STEP * te, D_MODEL), lambda t, e: (jnp.minimum(e, n_full - 1), 0)),
                  pl.BlockSpec((D_MODEL, 2 * te), lambda t, e: (0, jnp.maximum(2 * e - 1, 0))),
                  pl.BlockSpec((D_MODEL, 2 * te), lambda t, e: (0, jnp.minimum(2 * e, n_pairs - 1))),
                  slab_spec(lambda e: jnp.maximum(2 * e - 1, 0)), slab_spec(lambda e: jnp.minimum(2 * e, n_slabs - 1)),
                  slab_spec(lambda e: jnp.minimum(2 * e + 1, n_slabs - 1)),
                  slab_spec(lambda e: jnp.maximum(2 * e - 1, 0)), slab_spec(lambda e: jnp.minimum(2 * e, n_slabs - 1)),
                  slab_spec(lambda e: jnp.minimum(2 * e + 1, n_slabs - 1)),
                  sel_spec, sel_spec,
                  pl.BlockSpec((tn, D_MODEL), lambda t, e: (t, 0)),
                  pl.BlockSpec((1, D_MODEL), lambda t, e: (0, 0))],
        out_specs=pl.BlockSpec((tn, D_MODEL), lambda t, e: (t, 0)),
        out_shape=jax.ShapeDtypeStruct((n, D_MODEL), F32),
        scratch_shapes=[pltpu.VMEM((D_MODEL, tn), F32), pltpu.VMEM((te, tn), F32), pltpu.VMEM((te, tn), F32),
                        pltpu.VMEM((te, tn), BF16), pltpu.VMEM((te, tn), BF16)],
        compiler_params=pltpu.CompilerParams(dimension_semantics=("parallel", "arbitrary"),
                                             vmem_limit_bytes=VMEM_LIMIT),
        name="peer_dense",
    )(tok_t, u_b, v_t, v_t, a, a, a, l, l, l, b, c, x1, final_g)


def _row(v):
    return v.reshape(1, -1).astype(F32)


def kernel(x, norm1_g, w_in, rw_mu, rw_w0, rw_w_up, rw_a0, rw_a_up, rw_g_up, rw_k_k, rw_k_a, rw_r_k, rw_gn_w, rw_gn_b, ml_conv_q_w, ml_conv_q_b, ml_conv_k_w, ml_conv_k_b, ml_b_i, ml_b_f, ml_norm_w, gate_b, p_rw, p_ml, w_out, norm2_g, peer_w_q, peer_sub_keys, peer_u, peer_v, final_g):
    bsz, t_len, d = x.shape
    n = bsz * t_len
    x2 = x.reshape(n, d)
    l = 0

    w = w_in[l]
    ml0 = RW_COLS
    if0 = RW_COLS + ML_MAIN
    gt0 = if0 + 2 * ML_HEADS
    wrw = w[:, :ml0].astype(BF16)
    wml = w[:, ml0:if0].astype(BF16)
    wif = jnp.pad(w[:, if0:gt0], ((0, 0), (0, LANES - 2 * ML_HEADS))).astype(BF16)
    wgt = w[:, gt0:].astype(BF16)
    z_rw, z_ml, z_if, z_gate = _inproj(x2, _row(norm1_g[l]), wrw, wml, wif, wgt)

    zero_lora = jnp.zeros((RW_DECAY_LORA, RW_WIDTH), F32)
    wup = jnp.concatenate([rw_w_up[l], zero_lora], axis=0).astype(BF16)
    aup = jnp.concatenate([zero_lora, rw_a_up[l]], axis=0).astype(BF16)
    y_rw = _rwkv(z_rw.reshape(bsz, t_len, RW_COLS), _row(rw_mu[l]), _row(rw_w0[l]), wup, _row(rw_a0[l]), aup,
                 rw_g_up[l].astype(BF16), _row(rw_k_k[l]), _row(rw_k_a[l]), _row(rw_r_k[l]), _row(rw_gn_w[l]),
                 _row(rw_gn_b[l]))

    conv_w = jnp.concatenate([ml_conv_q_w[l], ml_conv_k_w[l]], axis=1)
    conv_b = jnp.concatenate([ml_conv_q_b[l], ml_conv_k_b[l]]).reshape(1, -1)
    b_if = jnp.pad(jnp.concatenate([ml_b_i[l], ml_b_f[l]]), (0, LANES - 2 * ML_HEADS)).reshape(1, -1)
    y_ml = _mlstm(z_ml.reshape(bsz, t_len, ML_MAIN), z_if.reshape(bsz, t_len, LANES), conv_w, conv_b, b_if,
                  _row(ml_norm_w[l]))

    x1, tok, q = _merge(x2, y_rw.reshape(n, RW_WIDTH), y_ml.reshape(n, ML_WIDTH), z_gate, _row(gate_b[l]),
                        p_rw[l].astype(BF16), p_ml[l].astype(BF16), w_out[l].astype(BF16), _row(norm2_g[l]),
                        peer_w_q[l].astype(BF16))

    a, cnt, b, c = _peer_select(q, peer_sub_keys[l])
    out = _peer_dense(tok.T, peer_u[l].astype(BF16), peer_v[l].T.astype(BF16), a, cnt, b, c, x1, _row(final_g))
    return out.reshape(bsz, t_len, d)
```
